```python
import math, functools
import jax, jax.numpy as jnp
from jax import lax
import numpy as np

D_MODEL = 2048
BATCH = 2
SEQ = 4096
DEPTH = 1
DEC_BATCH = 32
DEC_SEQ = 4
PAST_LEN = 16384
PAGE_SIZE = 128

MIX_WIDTH = D_MODEL
ATT_WIDTH = MIX_WIDTH // 2
RWKV_WIDTH = MIX_WIDTH - ATT_WIDTH
ATT_V_DIM = 128
ATT_QK_DIM = ATT_V_DIM // 2
ATT_HEADS = ATT_WIDTH // ATT_V_DIM
RWKV_HEAD = 64
RWKV_HEADS = RWKV_WIDTH // RWKV_HEAD
W_LORA = max(32, int(round(1.8 * RWKV_WIDTH ** 0.5 / 32)) * 32)
A_LORA = W_LORA
G_LORA = max(32, int(round(0.6 * RWKV_WIDTH ** 0.8 / 32)) * 32)
RWKV_PROJ = 3 * RWKV_WIDTH + W_LORA + A_LORA + G_LORA
IN_PROJ = 3 * ATT_WIDTH + RWKV_PROJ
RWKV_SPLITS = (RWKV_WIDTH, 2 * RWKV_WIDTH, 3 * RWKV_WIDTH,
               3 * RWKV_WIDTH + W_LORA, 3 * RWKV_WIDTH + W_LORA + A_LORA)
FFN_DIM = 4 * D_MODEL
N_BUCKETS = 32
MAX_DISTANCE = 128
Q_BLOCK = 128
ATT_SCALE = ATT_QK_DIM ** -0.5
RMS_EPS = 1e-6
SUBLN_EPS = 1e-5
LNX_EPS = 64e-5
NEG_INF = -1e30

kernel_name = "hymba_rwkv7_diffattn_decode_step"


def rms_norm(x, g, eps=RMS_EPS):
    xf = x.astype(jnp.float32)
    y = xf * lax.rsqrt(jnp.mean(xf * xf, axis=-1, keepdims=True) + eps)
    return (y * g.astype(jnp.float32)).astype(x.dtype)


def t5_bucket(dist):
    max_exact = N_BUCKETS // 2
    d = jnp.maximum(dist, 0)
    log_ratio = jnp.log(jnp.maximum(d, 1).astype(jnp.float32) / max_exact) / math.log(MAX_DISTANCE / max_exact)
    large = jnp.minimum(max_exact + (log_ratio * (N_BUCKETS - max_exact)).astype(jnp.int32), N_BUCKETS - 1)
    return jnp.where(d < max_exact, d, large)


def rel_bias(q_pos, k_pos, table):
    b = t5_bucket(q_pos[:, None] - k_pos[None, :])
    return jnp.transpose(table[b].astype(jnp.float32), (2, 0, 1))


def diff_attn_core(q, k, v, bias, mask, lam, subln_g, lambda_init):
    s = jnp.einsum("...qhme,...khme->...hmqk", q, k).astype(jnp.float32) * ATT_SCALE + bias[:, None]
    s = jnp.where(mask, s, NEG_INF)
    p = jax.nn.softmax(s, axis=-1)
    p = p[..., 0, :, :] - lam * p[..., 1, :, :]
    o = jnp.einsum("...hqk,...khd->...qhd", p.astype(v.dtype), v)
    return (rms_norm(o, subln_g, SUBLN_EPS) * (1.0 - lambda_init)).astype(v.dtype)


def prompt_diff_attn(q, k, v, bias_table, lam, subln_g, lambda_init):
    B, S = q.shape[0], q.shape[1]
    nb = S // Q_BLOCK
    qb = q.reshape(B, nb, Q_BLOCK, ATT_HEADS, 2, ATT_QK_DIM).swapaxes(0, 1)
    k_pos = jnp.arange(S)

    def block(args):
        q_blk, start = args
        q_pos = start + jnp.arange(Q_BLOCK)
        bias = rel_bias(q_pos, k_pos, bias_table)
        mask = k_pos[None, :] <= q_pos[:, None]
        return diff_attn_core(q_blk, k, v, bias, mask, lam, subln_g, lambda_init)

    o = lax.map(block, (qb, jnp.arange(nb) * Q_BLOCK))
    return o.swapaxes(0, 1).reshape(B, S, ATT_WIDTH)


def sample_diff_attn(q, k, v, cache_k, cache_v, layer, page_table, bias_table, lam, subln_g, lambda_init):
    DB, T = q.shape[0], q.shape[1]
    past = page_table.shape[1] * cache_k.shape[2]
    q_pos = past + jnp.arange(T)
    k_pos = jnp.arange(past + T)
    bias = rel_bias(q_pos, k_pos, bias_table)
    mask = k_pos[None, :] <= q_pos[:, None]

    def one(args):
        q_s, k_s, v_s, pt = args
        k_past = cache_k[layer, pt].reshape(past, ATT_HEADS, 2, ATT_QK_DIM).astype(k_s.dtype)
        v_past = cache_v[layer, pt].reshape(past, ATT_HEADS, ATT_V_DIM).astype(v_s.dtype)
        keys = jnp.concatenate([k_past, k_s], axis=0)
        vals = jnp.concatenate([v_past, v_s], axis=0)
        return diff_attn_core(q_s, keys, vals, bias, mask, lam, subln_g, lambda_init)

    o = lax.map(one, (q, k, v, page_table))
    return o.reshape(DB, T, ATT_WIDTH)


def wkv_scan(S0, r, w, k, v, a, b):
    def step(S, inp):
        r_t, w_t, k_t, v_t, a_t, b_t = inp
        sa = jnp.einsum("bhij,bhj->bhi", S, a_t)
        S = S * w_t[:, :, None, :] + sa[..., None] * b_t[:, :, None, :] + v_t[..., None] * k_t[:, :, None, :]
        return S, jnp.einsum("bhij,bhj->bhi", S, r_t)
    xs = tuple(jnp.moveaxis(t, 1, 0) for t in (r, w, k, v, a, b))
    S, ys = lax.scan(step, S0, xs)
    return S, jnp.moveaxis(ys, 0, 1)


def rwkv7_mix(p, p_prev, S0, lw):
    f32 = jnp.float32
    B, T = p.shape[0], p.shape[1]
    p_shift = jnp.concatenate([p_prev[:, None, :].astype(p.dtype), p[:, :-1]], axis=1)
    pm = p + lw["mu_shift"] * (p_shift - p)
    r, kr, v, wd, ad, gd = jnp.split(pm, RWKV_SPLITS, axis=-1)
    w = -jax.nn.softplus(-(lw["w0"] + jnp.tanh(wd) @ lw["w_lora_w"]).astype(f32)) - 0.5
    decay = jnp.exp(-jnp.exp(w))
    a = jax.nn.sigmoid((lw["a0"] + ad @ lw["w_lora_a"]).astype(f32))
    g = (jax.nn.sigmoid(gd) @ lw["w_lora_g"]).astype(f32)
    hs = (B, T, RWKV_HEADS, RWKV_HEAD)
    r_h = r.astype(f32).reshape(hs)
    k_h = kr.astype(f32).reshape(hs)
    v_h = v.astype(f32).reshape(hs)
    a_h = a.reshape(hs)
    d_h = decay.reshape(hs)
    kk = k_h * lw["k_k"].astype(f32).reshape(RWKV_HEADS, RWKV_HEAD)
    kk = kk / jnp.maximum(jnp.sqrt(jnp.sum(kk * kk, axis=-1, keepdims=True)), 1e-12)
    k_h = k_h * (1.0 + (a_h - 1.0) * lw["k_a"].astype(f32).reshape(RWKV_HEADS, RWKV_HEAD))
    S, y = wkv_scan(S0.astype(f32), r_h, d_h, k_h, v_h, -kk, kk * a_h)
    mu = jnp.mean(y, axis=-1, keepdims=True)
    var = jnp.mean(jnp.square(y - mu), axis=-1, keepdims=True)
    y = ((y - mu) * lax.rsqrt(var + LNX_EPS)).reshape(B, T, RWKV_WIDTH)
    y = y * lw["lnx_g"].astype(f32) + lw["lnx_b"].astype(f32)
    bonus = jnp.sum(r_h * k_h * lw["r_k"].astype(f32), axis=-1, keepdims=True) * v_h
    out = (y + bonus.reshape(B, T, RWKV_WIDTH)) * g
    return out.astype(p.dtype), S, p[:, -1]


def trunk_layer(x, c, attend, S0, p_prev, lw):
    B, T = x.shape[0], x.shape[1]
    mod = jax.nn.silu(c) @ lw["w_ada"] + lw["b_ada"]
    sh1, sc1, gt1, sh2, sc2, gt2 = jnp.split(mod[:, None, :], 6, axis=-1)
    h = rms_norm(x, lw["g_pre_mix"]) * (1.0 + sc1) + sh1
    proj = h @ lw["w_in"]
    q = proj[..., :ATT_WIDTH].reshape(B, T, ATT_HEADS, 2, ATT_QK_DIM)
    k = proj[..., ATT_WIDTH:2 * ATT_WIDTH].reshape(B, T, ATT_HEADS, 2, ATT_QK_DIM)
    v = proj[..., 2 * ATT_WIDTH:3 * ATT_WIDTH].reshape(B, T, ATT_HEADS, ATT_V_DIM)
    att = attend(q, k, v)
    rw, S, p_last = rwkv7_mix(proj[..., 3 * ATT_WIDTH:], p_prev, S0, lw)
    mix = jnp.concatenate([att.astype(rw.dtype), rw], axis=-1) @ lw["w_out"]
    x = x + gt1 * rms_norm(mix, lw["g_post_mix"])
    h = rms_norm(x, lw["g_pre_ffn"]) * (1.0 + sc2) + sh2
    f = jnp.square(jax.nn.relu(h @ lw["w_ffn_up"])) @ lw["w_ffn_down"]
    x = x + gt2 * rms_norm(f, lw["g_post_ffn"])
    return x, k.reshape(B, T, ATT_HEADS, 2 * ATT_QK_DIM), v, S, p_last


def setup_inputs(seed: int = 0) -> dict:
    key = jax.random.key(seed)
    ks = iter(jax.random.split(key, 48))

    def nrm(shape, s=1.0):
        return s * jax.random.normal(next(ks), shape, jnp.float32)

    n_pages = PAST_LEN // PAGE_SIZE
    n_used = DEC_BATCH * n_pages
    n_pool = n_used + n_used // 4
    page_table = jax.random.permutation(next(ks), n_pool)[:n_used].reshape(DEC_BATCH, n_pages).astype(jnp.int32)
    L, D = DEPTH, D_MODEL
    return {
        "x_prompt": nrm((BATCH, SEQ, D)),
        "x_sample": nrm((DEC_BATCH, DEC_SEQ, D)),
        "cache_k": nrm((L, n_pool, PAGE_SIZE, ATT_HEADS, 2 * ATT_QK_DIM)),
        "cache_v": nrm((L, n_pool, PAGE_SIZE, ATT_HEADS, ATT_V_DIM)),
        "state_wkv": nrm((L, DEC_BATCH, RWKV_HEADS, RWKV_HEAD, RWKV_HEAD), 0.5),
        "state_shift": nrm((L, DEC_BATCH, RWKV_PROJ)),
        "page_table": page_table,
        "c_prompt": nrm((BATCH, D)),
        "c_sample": nrm((DEC_BATCH, D)),
        "bias_table": nrm((N_BUCKETS, ATT_HEADS), 0.5),
        "w_ada": nrm((L, D, 6 * D), 0.5 * D ** -0.5),
        "b_ada": nrm((L, 6 * D), 0.02),
        "g_pre_mix": 1.0 + nrm((L, D), 0.02),
        "g_post_mix": 1.0 + nrm((L, D), 0.02),
        "g_pre_ffn": 1.0 + nrm((L, D), 0.02),
        "g_post_ffn": 1.0 + nrm((L, D), 0.02),
        "w_in": nrm((L, D, IN_PROJ), D ** -0.5),
        "mu_shift": jax.random.uniform(next(ks), (L, RWKV_PROJ), jnp.float32),
        "w0": -1.0 + nrm((L, RWKV_WIDTH), 0.5),
        "w_lora_w": nrm((L, W_LORA, RWKV_WIDTH), 0.5 * W_LORA ** -0.5),
        "a0": nrm((L, RWKV_WIDTH), 0.5),
        "w_lora_a": nrm((L, A_LORA, RWKV_WIDTH), 0.5 * A_LORA ** -0.5),
        "w_lora_g": nrm((L, G_LORA, RWKV_WIDTH), G_LORA ** -0.5),
        "k_k": 0.85 + nrm((L, RWKV_WIDTH), 0.05),
        "k_a": 1.0 + nrm((L, RWKV_WIDTH), 0.05),
        "r_k": nrm((L, RWKV_HEADS, RWKV_HEAD), 0.1),
        "lnx_g": 1.0 + nrm((L, RWKV_WIDTH), 0.02),
        "lnx_b": nrm((L, RWKV_WIDTH), 0.02),
        "lam_q1": nrm((L, ATT_QK_DIM), 0.1),
        "lam_k1": nrm((L, ATT_QK_DIM), 0.1),
        "lam_q2": nrm((L, ATT_QK_DIM), 0.1),
        "lam_k2": nrm((L, ATT_QK_DIM), 0.1),
        "subln_g": 1.0 + nrm((L, ATT_V_DIM), 0.02),
        "w_out": nrm((L, MIX_WIDTH, D), MIX_WIDTH ** -0.5),
        "w_ffn_up": nrm((L, D, FFN_DIM), D ** -0.5),
        "w_ffn_down": nrm((L, FFN_DIM, D), FFN_DIM ** -0.5),
    }


def reference(x_prompt, x_sample, cache_k, cache_v, state_wkv, state_shift, page_table,
              c_prompt, c_sample, bias_table, w_ada, b_ada, g_pre_mix, g_post_mix,
              g_pre_ffn, g_post_ffn, w_in, mu_shift, w0, w_lora_w, a0, w_lora_a, w_lora_g,
              k_k, k_a, r_k, lnx_g, lnx_b, lam_q1, lam_k1, lam_q2, lam_k2, subln_g,
              w_out, w_ffn_up, w_ffn_down):
    f32 = jnp.float32
    y_p, y_s = x_prompt, x_sample
    kp_l, vp_l, sp_l, hp_l, ks_l, vs_l, ss_l, hs_l = [], [], [], [], [], [], [], []
    for l in range(DEPTH):
        lw = {
            "w_ada": w_ada[l], "b_ada": b_ada[l],
            "g_pre_mix": g_pre_mix[l], "g_post_mix": g_post_mix[l],
            "g_pre_ffn": g_pre_ffn[l], "g_post_ffn": g_post_ffn[l],
            "w_in": w_in[l], "mu_shift": mu_shift[l], "w0": w0[l], "w_lora_w": w_lora_w[l],
            "a0": a0[l], "w_lora_a": w_lora_a[l], "w_lora_g": w_lora_g[l],
            "k_k": k_k[l], "k_a": k_a[l], "r_k": r_k[l], "lnx_g": lnx_g[l], "lnx_b": lnx_b[l],
            "w_out": w_out[l], "w_ffn_up": w_ffn_up[l], "w_ffn_down": w_ffn_down[l],
        }
        lambda_init = 0.8 - 0.6 * math.exp(-0.3 * l)
        lam = (jnp.exp(jnp.sum(lam_q1[l] * lam_k1[l]).astype(f32))
               - jnp.exp(jnp.sum(lam_q2[l] * lam_k2[l]).astype(f32)) + lambda_init)
        attend_p = functools.partial(prompt_diff_attn, bias_table=bias_table, lam=lam,
                                     subln_g=subln_g[l], lambda_init=lambda_init)
        attend_s = functools.partial(sample_diff_attn, cache_k=cache_k, cache_v=cache_v, layer=l,
                                     page_table=page_table, bias_table=bias_table, lam=lam,
                                     subln_g=subln_g[l], lambda_init=lambda_init)
        bp = y_p.shape[0]
        s0_p = jnp.zeros((bp, RWKV_HEADS, RWKV_HEAD, RWKV_HEAD), f32)
        h0_p = jnp.zeros((bp, RWKV_PROJ), y_p.dtype)
        y_p, k_p, v_p, s_p, h_p = trunk_layer(y_p, c_prompt, attend_p, s0_p, h0_p, lw)
        y_s, k_s, v_s, s_s, h_s = trunk_layer(y_s, c_sample, attend_s, state_wkv[l], state_shift[l], lw)
        kp_l.append(k_p); vp_l.append(v_p); sp_l.append(s_p.astype(state_wkv.dtype)); hp_l.append(h_p)
        ks_l.append(k_s); vs_l.append(v_s); ss_l.append(s_s.astype(state_wkv.dtype)); hs_l.append(h_s)
    return (y_p, y_s, jnp.stack(kp_l), jnp.stack(vp_l), jnp.stack(sp_l), jnp.stack(hp_l),
            jnp.stack(ks_l), jnp.stack(vs_l), jnp.stack(ss_l), jnp.stack(hs_l))
```

```python
import functools
import math

import numpy as np
import jax
import jax.numpy as jnp
from jax import lax
from jax.experimental import pallas as pl
from jax.experimental.pallas import tpu as pltpu

F32 = jnp.float32
BF16 = jnp.bfloat16

D_MODEL = 2048
ATT_WIDTH = 1024
RWKV_WIDTH = 1024
ATT_HEADS = 8
HEAD_V = 128
HEAD_QK = 64
RWKV_HEAD = 64
RWKV_PAIRS = RWKV_WIDTH // 128
RWKV_PROJ = 3360
P_PAD = 3584
LORA_IN = 3072
GATE_IN = 3200
FFN_DIM = 8192
N_BUCKETS = 32
MAX_DISTANCE = 128
PAGE = 128
ATT_SCALE = HEAD_QK ** -0.5
RMS_EPS = 1e-6
SUBLN_EPS = 1e-5
LNX_EPS = 64e-5
NEG_INF = -1e30
LAMBDA_INIT = 0.8 - 0.6 * math.exp(-0.3 * 0)
CHUNK = 64
VMEM_LIMIT = 56 * 1024 * 1024


def _params(*sem):
    return pltpu.CompilerParams(dimension_semantics=sem, vmem_limit_bytes=VMEM_LIMIT)


def _dot(a, b):
    return jnp.dot(a, b, preferred_element_type=F32)


def _dot_nt(a, b):
    return lax.dot_general(a, b, (((1,), (1,)), ((), ())), preferred_element_type=F32)


def _dot_tn(a, b):
    return lax.dot_general(a, b, (((0,), (0,)), ((), ())), preferred_element_type=F32)


def _t5_bucket_np(dist):
    max_exact = N_BUCKETS // 2
    d = np.maximum(dist, 0)
    ratio = np.log(np.maximum(d, 1).astype(np.float32) / max_exact) / math.log(MAX_DISTANCE / max_exact)
    large = np.minimum(max_exact + (ratio * (N_BUCKETS - max_exact)).astype(np.int32), N_BUCKETS - 1)
    return np.where(d < max_exact, d, large).astype(np.int32)


def _table_lookup(bucket, tbl_ref, h):
    out = jnp.zeros(bucket.shape, F32)
    for b in range(N_BUCKETS):
        out = jnp.where(bucket == b, tbl_ref[b, h], out)
    return out


def _lam(lamv_ref):
    v = lamv_ref[...]
    s1 = jnp.sum(v[0:1] * v[1:2], axis=-1, keepdims=True)
    s2 = jnp.sum(v[2:3] * v[3:4], axis=-1, keepdims=True)
    return jnp.exp(s1) - jnp.exp(s2) + LAMBDA_INIT


def _rms(x, g, eps):
    return x * lax.rsqrt(jnp.mean(x * x, axis=-1, keepdims=True) + eps) * g


def _ada_kernel(c_ref, w_ref, b_ref, o_ref):
    c = c_ref[...]
    x = (c * jax.nn.sigmoid(c)).astype(BF16)
    o_ref[...] = _dot(x, w_ref[...].astype(BF16)) + b_ref[...]


def _ada(c_all, w_ada, b_ada):
    rows, n = c_all.shape[0], w_ada.shape[1]
    tn = 1536
    return pl.pallas_call(
        _ada_kernel,
        grid=(n // tn,),
        in_specs=[pl.BlockSpec((rows, D_MODEL), lambda j: (0, 0)),
                  pl.BlockSpec((D_MODEL, tn), lambda j: (0, j)),
                  pl.BlockSpec((1, tn), lambda j: (0, j))],
        out_specs=pl.BlockSpec((rows, tn), lambda j: (0, j)),
        out_shape=jax.ShapeDtypeStruct((rows, n), F32),
        compiler_params=_params("arbitrary"),
        name="ada_mod",
    )(c_all, w_ada, b_ada)


IN_TN = 512


def _inproj_kernel(x_ref, mod_ref, g_ref, w_ref, q_ref, k_ref, v_ref, p_ref, h_scr):
    j = pl.program_id(1)

    @pl.when(j == 0)
    def _():
        h = _rms(x_ref[...], g_ref[...], RMS_EPS) * (1.0 + mod_ref[0, 1]) + mod_ref[0, 0]
        h_scr[...] = h.astype(BF16)

    acc = _dot(h_scr[...], w_ref[...])

    @pl.when(j < 2)
    def _():
        q_ref[...] = (acc * ATT_SCALE).astype(BF16)

    @pl.when((j >= 2) & (j < 4))
    def _():
        k_ref[...] = acc

    @pl.when((j >= 4) & (j < 6))
    def _():
        v_ref[...] = acc

    @pl.when(j >= 6)
    def _():
        p_ref[...] = acc


def _inproj(x, mod4, g_pre, w_in_b, tm, tiles_per_mod):
    m = x.shape[0]
    r = mod4.shape[2]
    nj = w_in_b.shape[1] // IN_TN
    npj = P_PAD // IN_TN
    return pl.pallas_call(
        _inproj_kernel,
        grid=(m // tm, nj),
        in_specs=[pl.BlockSpec((tm, D_MODEL), lambda i, j: (i, 0)),
                  pl.BlockSpec((1, 6, r, D_MODEL), lambda i, j: (i // tiles_per_mod, 0, 0, 0)),
                  pl.BlockSpec((1, D_MODEL), lambda i, j: (0, 0)),
                  pl.BlockSpec((D_MODEL, IN_TN), lambda i, j: (0, j))],
        out_specs=[pl.BlockSpec((tm, IN_TN), lambda i, j: (i, jnp.clip(j, 0, 1))),
                   pl.BlockSpec((tm, IN_TN), lambda i, j: (i, jnp.clip(j - 2, 0, 1))),
                   pl.BlockSpec((tm, IN_TN), lambda i, j: (i, jnp.clip(j - 4, 0, 1))),
                   pl.BlockSpec((tm, IN_TN), lambda i, j: (i, jnp.clip(j - 6, 0, npj - 1)))],
        out_shape=[jax.ShapeDtypeStruct((m, ATT_WIDTH), BF16),
                   jax.ShapeDtypeStruct((m, ATT_WIDTH), F32),
                   jax.ShapeDtypeStruct((m, ATT_WIDTH), F32),
                   jax.ShapeDtypeStruct((m, P_PAD), F32)],
        scratch_shapes=[pltpu.VMEM((tm, D_MODEL), BF16)],
        compiler_params=_params("arbitrary", "arbitrary"),
        name="in_proj",
    )(x, mod4, g_pre, w_in_b)


ATT_T = 512
ATT_SUB = 128


def _pattn_kernel(tbl_ref, lamv_ref, bkt_ref, subg_ref, q_ref, k_ref, v_ref, o_ref,
                  bias_scr, m_scr, l_scr, acc_scr):
    h = pl.program_id(1)
    qi = pl.program_id(2)
    ki = pl.program_id(3)
    nsub = ATT_T // ATT_SUB

    @pl.when((qi == 0) & (ki == 0))
    def _():
        row = lax.broadcasted_iota(jnp.int32, (ATT_SUB, ATT_SUB), 0)
        col = lax.broadcasted_iota(jnp.int32, (ATT_SUB, ATT_SUB), 1)
        t0 = jnp.where(col > row, NEG_INF, _table_lookup(bkt_ref[0], tbl_ref, h))
        t1 = _table_lookup(bkt_ref[1], tbl_ref, h)
        far = jnp.full((ATT_SUB, ATT_SUB), tbl_ref[N_BUCKETS - 1, h], F32)
        masked = jnp.full((ATT_SUB, ATT_SUB), NEG_INF, F32)
        for rb in range(nsub):
            for cb in range(nsub):
                d = rb - cb
                diag = t0 if d == 0 else t1 if d == 1 else far if d >= 2 else masked
                off = t1 if (rb == 0 and cb == nsub - 1) else far
                rs, cs = slice(rb * ATT_SUB, (rb + 1) * ATT_SUB), slice(cb * ATT_SUB, (cb + 1) * ATT_SUB)
                bias_scr[0, rs, cs] = diag
                bias_scr[1, rs, cs] = off
                bias_scr[2, rs, cs] = far

    @pl.when(ki == 0)
    def _():
        m_scr[...] = jnp.full(m_scr.shape, NEG_INF, F32)
        l_scr[...] = jnp.zeros(l_scr.shape, F32)
        acc_scr[...] = jnp.zeros(acc_scr.shape, F32)

    @pl.when(ki <= qi)
    def _():
        q = q_ref[0]
        lane = lax.broadcasted_iota(jnp.int32, q.shape, 1)
        zero = jnp.zeros_like(q)
        kb = k_ref[0].astype(BF16)
        vb = v_ref[0].astype(BF16)
        bias = bias_scr[jnp.minimum(qi - ki, 2)]
        for m, qm in enumerate((jnp.where(lane < HEAD_QK, q, zero), jnp.where(lane >= HEAD_QK, q, zero))):
            s = _dot_nt(qm, kb) + bias
            m_old = m_scr[m]
            m_new = jnp.maximum(m_old, jnp.max(s, axis=-1, keepdims=True))
            alpha = jnp.exp(m_old - m_new)
            p = jnp.exp(s - m_new)
            l_scr[m] = alpha * l_scr[m] + jnp.sum(p, axis=-1, keepdims=True)
            acc_scr[m] = alpha * acc_scr[m] + _dot(p.astype(BF16), vb)
            m_scr[m] = m_new

    @pl.when(ki == qi)
    def _():
        o = acc_scr[0] / l_scr[0] - _lam(lamv_ref) * (acc_scr[1] / l_scr[1])
        o = _rms(o, subg_ref[...], SUBLN_EPS) * (1.0 - LAMBDA_INIT)
        o_ref[0] = o.astype(BF16)


def _prompt_attn(q, k, v, tbl, lamv, subg):
    b, s, _ = q.shape
    nq = s // ATT_T
    r = np.arange(ATT_SUB)
    dist = r[:, None] - r[None, :]
    bkt = jnp.asarray(np.stack([_t5_bucket_np(dist), _t5_bucket_np(dist + ATT_SUB)]))
    kv_spec = pl.BlockSpec((1, ATT_T, HEAD_V), lambda bi, h, qi, ki: (bi, jnp.minimum(ki, qi), h))
    return pl.pallas_call(
        _pattn_kernel,
        grid=(b, ATT_HEADS, nq, nq),
        in_specs=[pl.BlockSpec(memory_space=pltpu.SMEM),
                  pl.BlockSpec((4, HEAD_QK), lambda bi, h, qi, ki: (0, 0)),
                  pl.BlockSpec((2, ATT_SUB, ATT_SUB), lambda bi, h, qi, ki: (0, 0, 0)),
                  pl.BlockSpec((1, HEAD_V), lambda bi, h, qi, ki: (0, 0)),
                  pl.BlockSpec((1, ATT_T, HEAD_V), lambda bi, h, qi, ki: (bi, qi, h)),
                  kv_spec, kv_spec],
        out_specs=pl.BlockSpec((1, ATT_T, HEAD_V), lambda bi, h, qi, ki: (bi, qi, h)),
        out_shape=jax.ShapeDtypeStruct((b, s, ATT_WIDTH), BF16),
        scratch_shapes=[pltpu.VMEM((3, ATT_T, ATT_T), F32),
                        pltpu.VMEM((2, ATT_T, 1), F32),
                        pltpu.VMEM((2, ATT_T, 1), F32),
                        pltpu.VMEM((2, ATT_T, HEAD_V), F32)],
        compiler_params=_params("arbitrary", "arbitrary", "arbitrary", "arbitrary"),
        name="prompt_attn",
    )(tbl, lamv, bkt, subg, q, k, v)


PAGES_PER_STEP = 8
QROWS = 16


def _sattn_kernel(pt_ref, tbl_ref, lamv_ref, bkt_ref, subg_ref, qm_ref, kn_ref, vn_ref, *rest):
    g_pages = PAGES_PER_STEP
    k_refs = rest[:g_pages]
    v_refs = rest[g_pages:2 * g_pages]
    o_ref, bias_scr, m_scr, l_scr, acc_scr = rest[2 * g_pages:]
    del pt_ref
    j = pl.program_id(1)
    nj = pl.num_programs(1)
    n_new = kn_ref.shape[1]

    @pl.when(j == 0)
    def _():
        m_scr[...] = jnp.full(m_scr.shape, NEG_INF, F32)
        l_scr[...] = jnp.zeros(l_scr.shape, F32)
        acc_scr[...] = jnp.zeros(acc_scr.shape, F32)
        for h in range(ATT_HEADS):
            bias_scr[h] = jnp.full(bias_scr.shape[1:], tbl_ref[N_BUCKETS - 1, h], F32)

    @pl.when(j == nj - 1)
    def _():
        for h in range(ATT_HEADS):
            bias_scr[h] = _table_lookup(bkt_ref[...], tbl_ref, h)

    for h in range(ATT_HEADS):
        qm = qm_ref[0, h]
        kcat = jnp.concatenate([kr[0, 0, :, h, :] for kr in k_refs], axis=0).astype(BF16)
        vcat = jnp.concatenate([vr[0, 0, :, h, :] for vr in v_refs], axis=0).astype(BF16)
        s = _dot_nt(qm, kcat) + bias_scr[h]
        m_old = m_scr[h]
        m_new = jnp.maximum(m_old, jnp.max(s, axis=-1, keepdims=True))
        alpha = jnp.exp(m_old - m_new)
        p = jnp.exp(s - m_new)
        l_scr[h] = alpha * l_scr[h] + jnp.sum(p, axis=-1, keepdims=True)
        acc_scr[h] = alpha * acc_scr[h] + _dot(p.astype(BF16), vcat)
        m_scr[h] = m_new

    @pl.when(j == nj - 1)
    def _():
        lam = _lam(lamv_ref)
        t_row = lax.broadcasted_iota(jnp.int32, (QROWS, 1), 0) % 8
        for h in range(ATT_HEADS):
            hs = slice(h * HEAD_V, (h + 1) * HEAD_V)
            qf = qm_ref[0, h].astype(F32)
            m_run, l_run, acc = m_scr[h], l_scr[h], acc_scr[h]
            for tn in range(n_new):
                d = t_row - tn
                bias = jnp.full((QROWS, 1), NEG_INF, F32)
                for dd in range(n_new):
                    bias = jnp.where(d == dd, tbl_ref[dd, h], bias)
                s = jnp.sum(qf * kn_ref[0, tn:tn + 1, hs], axis=-1, keepdims=True) + bias
                m_new = jnp.maximum(m_run, s)
                alpha = jnp.exp(m_run - m_new)
                p = jnp.exp(s - m_new)
                l_run = alpha * l_run + p
                acc = alpha * acc + p * vn_ref[0, tn:tn + 1, hs]
                m_run = m_new
            on = acc / l_run
            o = on[0:8] - lam * on[8:16]
            o = _rms(o, subg_ref[...], SUBLN_EPS) * (1.0 - LAMBDA_INIT)
            o_ref[0, :, hs] = o[0:n_new]


def _sample_attn(qm, k_new, v_new, cache_k, cache_v, page_table, tbl, lamv, subg):
    db, t_new = k_new.shape[0], k_new.shape[1]
    n_pages = page_table.shape[1]
    g_pages = PAGES_PER_STEP
    gt = g_pages * PAGE
    t_row = (np.arange(QROWS) % 8)[:, None]
    bkt = jnp.asarray(_t5_bucket_np(gt + t_row - np.arange(gt)[None, :]))

    def page_spec(g):
        return pl.BlockSpec((1, 1, PAGE, ATT_HEADS, HEAD_V),
                            lambda b, j, pt: (0, pt[b, j * g_pages + g], 0, 0, 0))

    new_spec = pl.BlockSpec((1, t_new, ATT_WIDTH), lambda b, j, pt: (b, 0, 0))
    grid_spec = pltpu.PrefetchScalarGridSpec(
        num_scalar_prefetch=1,
        grid=(db, n_pages // g_pages),
        in_specs=[pl.BlockSpec(memory_space=pltpu.SMEM),
                  pl.BlockSpec((4, HEAD_QK), lambda b, j, pt: (0, 0)),
                  pl.BlockSpec((QROWS, gt), lambda b, j, pt: (0, 0)),
                  pl.BlockSpec((1, HEAD_V), lambda b, j, pt: (0, 0)),
                  pl.BlockSpec((1, ATT_HEADS, QROWS, HEAD_V), lambda b, j, pt: (b, 0, 0, 0)),
                  new_spec, new_spec]
                 + [page_spec(g) for g in range(g_pages)]
                 + [page_spec(g) for g in range(g_pages)],
        out_specs=new_spec,
        scratch_shapes=[pltpu.VMEM((ATT_HEADS, QROWS, gt), F32),
                        pltpu.VMEM((ATT_HEADS, QROWS, 1), F32),
                        pltpu.VMEM((ATT_HEADS, QROWS, 1), F32),
                        pltpu.VMEM((ATT_HEADS, QROWS, HEAD_V), F32)],
    )
    return pl.pallas_call(
        _sattn_kernel,
        grid_spec=grid_spec,
        out_shape=jax.ShapeDtypeStruct((db, t_new, ATT_WIDTH), F32),
        compiler_params=_params("arbitrary", "arbitrary"),
        name="sample_attn",
    )(page_table, tbl, lamv, bkt, subg, qm, k_new, v_new,
      *([cache_k] * g_pages), *([cache_v] * g_pages))


def _prep_math(p, ps, mu, w0, a0, kkw, kaw, rkw, wwa, wg, ones_bd, out_refs):
    pm = p + mu * (ps - p)
    r = pm[:, 0:RWKV_WIDTH]
    kr = pm[:, RWKV_WIDTH:2 * RWKV_WIDTH]
    v = pm[:, 2 * RWKV_WIDTH:3 * RWKV_WIDTH]
    wa = pm[:, LORA_IN:LORA_IN + 128]
    lane = lax.broadcasted_iota(jnp.int32, wa.shape, 1)
    la = _dot(jnp.where(lane < 64, jnp.tanh(wa), wa).astype(BF16), wwa)
    z = -(w0 + la[:, :RWKV_WIDTH])
    softplus = jnp.maximum(z, 0.0) + jnp.log(1.0 + jnp.exp(-jnp.abs(z)))
    log_decay = -jnp.exp(-softplus - 0.5)
    asig = jax.nn.sigmoid(a0 + la[:, RWKV_WIDTH:])
    g = _dot(jax.nn.sigmoid(pm[:, GATE_IN:P_PAD]).astype(BF16), wg)
    kk = kr * kkw
    norm = jnp.sqrt(_dot((kk * kk).astype(BF16), ones_bd))
    kk = kk / jnp.maximum(norm, 1e-12)
    k2 = kr * (1.0 + (asig - 1.0) * kaw)
    bonus = _dot((r * k2 * rkw).astype(BF16), ones_bd) * v
    r_ref, lw_ref, k_ref, v_ref, a_ref, b_ref, g_ref, bonus_ref = out_refs
    r_ref[...] = r
    lw_ref[...] = log_decay
    k_ref[...] = k2
    v_ref[...] = v
    a_ref[...] = -kk
    b_ref[...] = kk * asig
    g_ref[...] = g
    bonus_ref[...] = bonus


def _prep_carry_kernel(p_ref, mu, w0, a0, kkw, kaw, rkw, wwa, wg, ones_bd, *rest):
    out_refs, carry = rest[:8], rest[8]
    i = pl.program_id(1)

    @pl.when(i == 0)
    def _():
        carry[...] = jnp.zeros(carry.shape, F32)

    p = p_ref[...]
    row = lax.broadcasted_iota(jnp.int32, p.shape, 0)
    ps = jnp.where(row == 0, carry[...], pltpu.roll(p, 1, 0))
    carry[...] = p[p.shape[0] - 1:, :]
    _prep_math(p, ps, mu[...], w0[...], a0[...], kkw[...], kaw[...], rkw[...], wwa[...], wg[...],
               ones_bd[...], out_refs)


def _prep_shift_kernel(p_ref, ps_ref, mu, w0, a0, kkw, kaw, rkw, wwa, wg, ones_bd, *out_refs):
    _prep_math(p_ref[...], ps_ref[...], mu[...], w0[...], a0[...], kkw[...], kaw[...], rkw[...], wwa[...],
               wg[...], ones_bd[...], out_refs)


def _rwkv_prep(p, p_shift, consts, n_seq, tm):
    m = p.shape[0]
    per_seq = m // n_seq // tm if p_shift is None else 0
    if p_shift is None:
        grid = (n_seq, per_seq)
        tile = lambda w: pl.BlockSpec((tm, w), lambda s, i: (s * per_seq + i, 0))
        const = lambda a: pl.BlockSpec(a.shape, lambda s, i: (0,) * a.ndim)
        kern, args, sem = _prep_carry_kernel, (p,), ("arbitrary", "arbitrary")
        scratch = [pltpu.VMEM((1, P_PAD), F32)]
    else:
        grid = (m // tm,)
        tile = lambda w: pl.BlockSpec((tm, w), lambda i: (i, 0))
        const = lambda a: pl.BlockSpec(a.shape, lambda i: (0,) * a.ndim)
        kern, args, sem = _prep_shift_kernel, (p, p_shift), ("arbitrary",)
        scratch = []
    return pl.pallas_call(
        kern,
        grid=grid,
        in_specs=[tile(P_PAD)] * len(args) + [const(a) for a in consts],
        out_specs=[tile(RWKV_WIDTH)] * 8,
        out_shape=[jax.ShapeDtypeStruct((m, RWKV_WIDTH), F32)] * 8,
        scratch_shapes=scratch,
        compiler_params=_params(*sem),
        name="rwkv_prep",
    )(*args, *consts)


def _wkv_kernel(r_ref, lw_ref, k_ref, v_ref, a_ref, b_ref, s0_ref, y_ref, sout_ref, s_scr):
    c = pl.program_id(1)
    nc = pl.num_programs(1)
    C = CHUNK

    @pl.when(c == 0)
    def _():
        s_scr[...] = s0_ref[0]

    lw = lw_ref[0]
    trow = lax.broadcasted_iota(jnp.int32, (C, C), 0)
    tcol = lax.broadcasted_iota(jnp.int32, (C, C), 1)
    tri = jnp.where(tcol <= trow, 1.0, 0.0).astype(BF16)
    h1 = lw.astype(BF16)
    r1 = lw - h1.astype(F32)
    h2 = r1.astype(BF16)
    h3 = (r1 - h2.astype(F32)).astype(BF16)
    cs = _dot(tri, jnp.concatenate([h1, h2, h3], axis=1))
    cum = cs[:, :RWKV_WIDTH] + cs[:, RWKV_WIDTH:2 * RWKV_WIDTH] + cs[:, 2 * RWKV_WIDTH:]
    tot = cum[C - 1:C, :]
    e_inv = jnp.exp(-cum)
    e_tail = jnp.exp(tot - cum)
    at_all = a_ref[0] * jnp.exp(cum - lw)
    rt_all = r_ref[0] * jnp.exp(cum)
    bt_all = b_ref[0] * e_inv
    kt_all = k_ref[0] * e_inv
    bh_all = b_ref[0] * e_tail
    kh_all = k_ref[0] * e_tail
    w_tot = jnp.exp(tot)
    v_all = v_ref[0]

    row = lax.broadcasted_iota(jnp.int32, (2 * C, 2 * C), 0)
    col = lax.broadcasted_iota(jnp.int32, (2 * C, 2 * C), 1)
    same = (row // C) == (col // C)
    strict = same & (col < row)
    incl = same & (col <= row)
    eye = jnp.where(row == col, 1.0, 0.0)
    lane = lax.broadcasted_iota(jnp.int32, (C, 128), 1)
    first = lane < RWKV_HEAD

    def stack(x):
        return jnp.concatenate([jnp.where(first, x, 0.0), jnp.where(first, 0.0, x)], axis=0)

    def dup(x):
        return jnp.concatenate([x, x], axis=0)

    for pi in range(RWKV_PAIRS):
        ls = slice(pi * 128, (pi + 1) * 128)
        at_s, rt_s, v_s = stack(at_all[:, ls]), stack(rt_all[:, ls]), stack(v_all[:, ls])
        g = _dot_nt(jnp.concatenate([at_s, rt_s], axis=0).astype(BF16),
                    jnp.concatenate([dup(bt_all[:, ls]), dup(kt_all[:, ls])], axis=0).astype(BF16))
        l_ab = jnp.where(strict, g[0:2 * C, 0:2 * C], 0.0)
        a_ak = jnp.where(strict, g[0:2 * C, 2 * C:4 * C], 0.0)
        a_rb = jnp.where(incl, g[2 * C:4 * C, 0:2 * C], 0.0)
        a_rk = jnp.where(incl, g[2 * C:4 * C, 2 * C:4 * C], 0.0)
        x = eye + l_ab
        pw = l_ab
        for _ in range(int(math.log2(C)) - 1):
            pwb = pw.astype(BF16)
            pw = _dot(pwb, pwb)
            x = x + _dot(x.astype(BF16), pw.astype(BF16))
        av = _dot(jnp.concatenate([a_ak, a_rk], axis=0).astype(BF16), v_s.astype(BF16))
        tx = _dot(x.astype(BF16), jnp.concatenate([at_s, av[0:2 * C]], axis=1).astype(BF16))
        s_old = s_scr[pi]
        az = _dot_nt(jnp.concatenate([tx[:, 0:128], rt_s], axis=0).astype(BF16), s_old.astype(BF16))
        u = az[0:2 * C] + tx[:, 128:256]
        y = az[2 * C:4 * C] + _dot(a_rb.astype(BF16), u.astype(BF16)) + av[2 * C:4 * C]
        y_ref[0, :, ls] = y[0:C] + y[C:2 * C]
        upd = _dot_tn(jnp.concatenate([u, v_s], axis=0).astype(BF16),
                      jnp.concatenate([stack(bh_all[:, ls]), stack(kh_all[:, ls])], axis=0).astype(BF16))
        s_scr[pi] = s_old * w_tot[:, ls] + upd

    @pl.when(c == nc - 1)
    def _():
        sout_ref[0] = s_scr[...]


def _wkv(r, lw, k, v, a, b, s0_bd):
    n_seq, t, _ = r.shape
    seq_spec = pl.BlockSpec((1, CHUNK, RWKV_WIDTH), lambda s, c: (s, c, 0))
    st_spec = pl.BlockSpec((1, RWKV_PAIRS, 128, 128), lambda s, c: (s, 0, 0, 0))
    return pl.pallas_call(
        _wkv_kernel,
        grid=(n_seq, t // CHUNK),
        in_specs=[seq_spec] * 6 + [st_spec],
        out_specs=[seq_spec, st_spec],
        out_shape=[jax.ShapeDtypeStruct((n_seq, t, RWKV_WIDTH), F32),
                   jax.ShapeDtypeStruct((n_seq, RWKV_PAIRS, 128, 128), F32)],
        scratch_shapes=[pltpu.VMEM((RWKV_PAIRS, 128, 128), F32)],
        compiler_params=_params("arbitrary", "arbitrary"),
        name="wkv_scan",
    )(r, lw, k, v, a, b, s0_bd)


def _outproj_kernel(att_ref, y_ref, g_ref, bonus_ref, lng_ref, lnb_ref, ones_ref, wo_ref, x_ref, mod_ref,
                    gpost_ref, o_ref):
    y = y_ref[...]
    ones_bd = ones_ref[...]
    inv_n = 1.0 / RWKV_HEAD
    yc = y - _dot(y.astype(BF16), ones_bd) * inv_n
    var = _dot((yc * yc).astype(BF16), ones_bd) * inv_n
    yn = yc * lax.rsqrt(var + LNX_EPS)
    rw = (yn * lng_ref[...] + lnb_ref[...] + bonus_ref[...]) * g_ref[...]
    mix = (_dot(att_ref[...].astype(BF16), wo_ref[0:ATT_WIDTH, :])
           + _dot(rw.astype(BF16), wo_ref[ATT_WIDTH:, :]))
    o_ref[...] = x_ref[...] + mod_ref[0, 2] * _rms(mix, gpost_ref[...], RMS_EPS)


def _outproj(att, y, g, bonus, lng, lnb, ones_bd, w_out_b, x, mod4, g_post, tm, tiles_per_mod):
    m = x.shape[0]
    r = mod4.shape[2]
    tile = lambda w: pl.BlockSpec((tm, w), lambda i: (i, 0))
    const = lambda a: pl.BlockSpec(a.shape, lambda i: (0,) * a.ndim)
    return pl.pallas_call(
        _outproj_kernel,
        grid=(m // tm,),
        in_specs=[tile(ATT_WIDTH)] * 4 + [const(lng), const(lnb), const(ones_bd), const(w_out_b),
                                          tile(D_MODEL),
                                          pl.BlockSpec((1, 6, r, D_MODEL), lambda i: (i // tiles_per_mod, 0, 0, 0)),
                                          const(g_post)],
        out_specs=tile(D_MODEL),
        out_shape=jax.ShapeDtypeStruct((m, D_MODEL), F32),
        compiler_params=_params("arbitrary"),
        name="out_proj",
    )(att, y, g, bonus, lng, lnb, ones_bd, w_out_b, x, mod4, g_post)


FFN_TF = 1024


def _ffn_kernel(x_ref, mod_ref, gpre_ref, gpost_ref, wu_ref, wd_ref, o_ref, h_scr, acc_scr):
    f = pl.program_id(1)

    @pl.when(f == 0)
    def _():
        h = _rms(x_ref[...], gpre_ref[...], RMS_EPS) * (1.0 + mod_ref[0, 4]) + mod_ref[0, 3]
        h_scr[...] = h.astype(BF16)
        acc_scr[...] = jnp.zeros(acc_scr.shape, F32)

    u = jnp.maximum(_dot(h_scr[...], wu_ref[...]), 0.0)
    acc_scr[...] += _dot((u * u).astype(BF16), wd_ref[...])

    @pl.when(f == pl.num_programs(1) - 1)
    def _():
        o_ref[...] = x_ref[...] + mod_ref[0, 5] * _rms(acc_scr[...], gpost_ref[...], RMS_EPS)


def _ffn(x, mod4, g_pre, g_post, w_up_b, w_down_b, tm, tiles_per_mod):
    m = x.shape[0]
    r = mod4.shape[2]
    return pl.pallas_call(
        _ffn_kernel,
        grid=(m // tm, FFN_DIM // FFN_TF),
        in_specs=[pl.BlockSpec((tm, D_MODEL), lambda i, f: (i, 0)),
                  pl.BlockSpec((1, 6, r, D_MODEL), lambda i, f: (i // tiles_per_mod, 0, 0, 0)),
                  pl.BlockSpec((1, D_MODEL), lambda i, f: (0, 0)),
                  pl.BlockSpec((1, D_MODEL), lambda i, f: (0, 0)),
                  pl.BlockSpec((D_MODEL, FFN_TF), lambda i, f: (0, f)),
                  pl.BlockSpec((FFN_TF, D_MODEL), lambda i, f: (f, 0))],
        out_specs=pl.BlockSpec((tm, D_MODEL), lambda i, f: (i, 0)),
        out_shape=jax.ShapeDtypeStruct((m, D_MODEL), F32),
        scratch_shapes=[pltpu.VMEM((tm, D_MODEL), BF16), pltpu.VMEM((tm, D_MODEL), F32)],
        compiler_params=_params("arbitrary", "arbitrary"),
        name="ffn",
    )(x, mod4, g_pre, g_post, w_up_b, w_down_b)


def _state_to_blockdiag(s):
    n = s.shape[0]
    s = s.reshape(n, RWKV_PAIRS, 2, RWKV_HEAD, RWKV_HEAD)
    z = jnp.zeros((n, RWKV_PAIRS, RWKV_HEAD, RWKV_HEAD), s.dtype)
    top = jnp.concatenate([s[:, :, 0], z], axis=-1)
    bot = jnp.concatenate([z, s[:, :, 1]], axis=-1)
    return jnp.concatenate([top, bot], axis=-2)


def _blockdiag_to_state(sbd):
    n = sbd.shape[0]
    s = jnp.stack([sbd[:, :, :RWKV_HEAD, :RWKV_HEAD], sbd[:, :, RWKV_HEAD:, RWKV_HEAD:]], axis=2)
    return s.reshape(n, 2 * RWKV_PAIRS, RWKV_HEAD, RWKV_HEAD)


def _pad_cols(a, width):
    return jnp.pad(a, ((0, 0), (0, width - a.shape[1])))


def kernel(x_prompt, x_sample, cache_k, cache_v, state_wkv, state_shift, page_table, c_prompt, c_sample, bias_table, w_ada, b_ada, g_pre_mix, g_post_mix, g_pre_ffn, g_post_ffn, w_in, mu_shift, w0, w_lora_w, a0, w_lora_a, w_lora_g, k_k, k_a, r_k, lnx_g, lnx_b, lam_q1, lam_k1, lam_q2, lam_k2, subln_g, w_out, w_ffn_up, w_ffn_down):
    bsz, seq, d = x_prompt.shape
    dbs, tdec, _ = x_sample.shape
    n_p, n_s = bsz * seq, dbs * tdec

    w_in_b = jnp.concatenate([w_in[0], jnp.zeros((d, P_PAD - RWKV_PROJ), F32)], axis=1).astype(BF16)
    w_out_b = w_out[0].astype(BF16)
    w_up_b = w_ffn_up[0].astype(BF16)
    w_down_b = w_ffn_down[0].astype(BF16)
    wwa = jnp.zeros((128, 2 * RWKV_WIDTH), F32)
    wwa = wwa.at[:64, :RWKV_WIDTH].set(w_lora_w[0]).at[64:, RWKV_WIDTH:].set(w_lora_a[0]).astype(BF16)
    wg = jnp.pad(w_lora_g[0], ((0, P_PAD - GATE_IN - w_lora_g.shape[1]), (0, 0))).astype(BF16)
    head_id = jnp.arange(RWKV_WIDTH) // RWKV_HEAD
    ones_bd = (head_id[:, None] == head_id[None, :]).astype(BF16)
    prep_consts = (_pad_cols(mu_shift, P_PAD), w0, a0, k_k, k_a, r_k.reshape(1, RWKV_WIDTH), wwa, wg, ones_bd)
    lamv = jnp.concatenate([lam_q1, lam_k1, lam_q2, lam_k2], axis=0)

    n_c = bsz + dbs
    c_all = jnp.pad(jnp.concatenate([c_prompt, c_sample], axis=0), ((0, (-n_c) % 8), (0, 0)))
    mod = _ada(c_all, w_ada[0], b_ada)
    mod_p = mod[:bsz].reshape(bsz, 6, 1, d)
    mod_s = jnp.repeat(mod[bsz:n_c].reshape(dbs, 6, d), tdec, axis=0).transpose(1, 0, 2).reshape(1, 6, n_s, d)

    xp = x_prompt.reshape(n_p, d)
    xs = x_sample.reshape(n_s, d)
    tm_p = min(1024, seq)
    q_p, k_p, v_p, p_p = _inproj(xp, mod_p, g_pre_mix, w_in_b, tm_p, seq // tm_p)
    q_s, k_s, v_s, p_s = _inproj(xs, mod_s, g_pre_mix, w_in_b, n_s, 1)

    att_p = _prompt_attn(q_p.reshape(bsz, seq, ATT_WIDTH), k_p.reshape(bsz, seq, ATT_WIDTH),
                         v_p.reshape(bsz, seq, ATT_WIDTH), bias_table, lamv, subln_g)
    q4 = q_s.reshape(dbs, tdec, ATT_HEADS, HEAD_V).transpose(0, 2, 1, 3)
    first = jnp.arange(HEAD_V) < HEAD_QK
    zq = jnp.zeros((dbs, ATT_HEADS, 8 - tdec, HEAD_V), BF16)
    qm = jnp.concatenate([jnp.where(first, q4, 0), zq, jnp.where(first, 0, q4), zq], axis=2)
    att_s = _sample_attn(qm, k_s.reshape(dbs, tdec, ATT_WIDTH), v_s.reshape(dbs, tdec, ATT_WIDTH),
                         cache_k, cache_v, page_table, bias_table, lamv, subln_g)

    pre_p = _rwkv_prep(p_p, None, prep_consts, bsz, min(256, seq))
    p_s3 = p_s.reshape(dbs, tdec, P_PAD)
    shift_s = jnp.concatenate([_pad_cols(state_shift[0], P_PAD)[:, None, :], p_s3[:, :-1]], axis=1)
    pre_s = _rwkv_prep(p_s, shift_s.reshape(n_s, P_PAD), prep_consts, dbs, n_s)
    r_p, lw_p, kk_p, vv_p, a_p, b_p, g_p, bonus_p = pre_p
    r_s, lw_s, kk_s, vv_s, a_s, b_s, g_s, bonus_s = pre_s
    seq3 = lambda t: t.reshape(bsz, seq, RWKV_WIDTH)
    y_p, sbd_p = _wkv(seq3(r_p), seq3(lw_p), seq3(kk_p), seq3(vv_p), seq3(a_p), seq3(b_p),
                      jnp.zeros((bsz, RWKV_PAIRS, 128, 128), F32))
    dec3 = lambda t: jnp.pad(t.reshape(dbs, tdec, RWKV_WIDTH), ((0, 0), (0, CHUNK - tdec), (0, 0)))
    y_s, sbd_s = _wkv(dec3(r_s), dec3(lw_s), dec3(kk_s), dec3(vv_s), dec3(a_s), dec3(b_s),
                      _state_to_blockdiag(state_wkv[0]))
    y_s = y_s[:, :tdec].reshape(n_s, RWKV_WIDTH)

    tm_o = min(256, seq)
    x1_p = _outproj(att_p.reshape(n_p, ATT_WIDTH), y_p.reshape(n_p, RWKV_WIDTH), g_p, bonus_p, lnx_g, lnx_b,
                    ones_bd, w_out_b, xp, mod_p, g_post_mix, tm_o, seq // tm_o)
    x1_s = _outproj(att_s.reshape(n_s, ATT_WIDTH), y_s, g_s, bonus_s, lnx_g, lnx_b,
                    ones_bd, w_out_b, xs, mod_s, g_post_mix, n_s, 1)
    tm_f = min(512, seq)
    out_p = _ffn(x1_p, mod_p, g_pre_ffn, g_post_ffn, w_up_b, w_down_b, tm_f, seq // tm_f)
    out_s = _ffn(x1_s, mod_s, g_pre_ffn, g_post_ffn, w_up_b, w_down_b, n_s, 1)

    return (out_p.reshape(bsz, seq, d),
            out_s.reshape(dbs, tdec, d),
            k_p.reshape(1, bsz, seq, ATT_HEADS, HEAD_V),
            v_p.reshape(1, bsz, seq, ATT_HEADS, HEAD_V),
            _blockdiag_to_state(sbd_p)[None],
            p_p.reshape(bsz, seq, P_PAD)[None, :, -1, :RWKV_PROJ],
            k_s.reshape(1, dbs, tdec, ATT_HEADS, HEAD_V),
            v_s.reshape(1, dbs, tdec, ATT_HEADS, HEAD_V),
            _blockdiag_to_state(sbd_s)[None],
            p_s3[None, :, -1, :RWKV_PROJ])
```

```python
import functools
import math

import numpy as np
import jax
import jax.numpy as jnp
from jax import lax
from jax.experimental import pallas as pl
from jax.experimental.pallas import tpu as pltpu

F32 = jnp.float32
BF16 = jnp.bfloat16

D_MODEL = 2048
ATT_WIDTH = 1024
RWKV_WIDTH = 1024
ATT_HEADS = 8
HEAD_V = 128
HEAD_QK = 64
RWKV_HEAD = 64
RWKV_PAIRS = RWKV_WIDTH // 128
RWKV_PROJ = 3360
P_PAD = 3584
LORA_IN = 3072
GATE_IN = 3200
FFN_DIM = 8192
N_BUCKETS = 32
MAX_DISTANCE = 128
PAGE = 128
ATT_SCALE = HEAD_QK ** -0.5
RMS_EPS = 1e-6
SUBLN_EPS = 1e-5
LNX_EPS = 64e-5
NEG_INF = -1e30
LAMBDA_INIT = 0.8 - 0.6 * math.exp(-0.3 * 0)
CHUNK = 64
VMEM_LIMIT = 56 * 1024 * 1024


def _params(*sem):
    return pltpu.CompilerParams(dimension_semantics=sem, vmem_limit_bytes=VMEM_LIMIT)


def _dot(a, b):
    return jnp.dot(a, b, preferred_element_type=F32)


def _dot_nt(a, b):
    return lax.dot_general(a, b, (((1,), (1,)), ((), ())), preferred_element_type=F32)


def _dot_tn(a, b):
    return lax.dot_general(a, b, (((0,), (0,)), ((), ())), preferred_element_type=F32)


def _t5_bucket_np(dist):
    max_exact = N_BUCKETS // 2
    d = np.maximum(dist, 0)
    ratio = np.log(np.maximum(d, 1).astype(np.float32) / max_exact) / math.log(MAX_DISTANCE / max_exact)
    large = np.minimum(max_exact + (ratio * (N_BUCKETS - max_exact)).astype(np.int32), N_BUCKETS - 1)
    return np.where(d < max_exact, d, large).astype(np.int32)


def _table_lookup(bucket, tbl_ref, h):
    out = jnp.zeros(bucket.shape, F32)
    for b in range(N_BUCKETS):
        out = jnp.where(bucket == b, tbl_ref[b, h], out)
    return out


def _lam(lamv_ref):
    v = lamv_ref[...]
    s1 = jnp.sum(v[0:1] * v[1:2], axis=-1, keepdims=True)
    s2 = jnp.sum(v[2:3] * v[3:4], axis=-1, keepdims=True)
    return jnp.exp(s1) - jnp.exp(s2) + LAMBDA_INIT


def _rms(x, g, eps):
    return x * lax.rsqrt(jnp.mean(x * x, axis=-1, keepdims=True) + eps) * g


def _lanes(x, reps):
    return jnp.concatenate([x] * reps, axis=1)


def _ada_kernel(c_ref, w_ref, b_ref, o_ref):
    c = c_ref[...]
    x = (c * jax.nn.sigmoid(c)).astype(BF16)
    o_ref[...] = _dot(x, w_ref[...].astype(BF16)) + b_ref[...]


def _ada(c_all, w_ada, b_ada):
    rows, n = c_all.shape[0], w_ada.shape[1]
    tn = 1536
    return pl.pallas_call(
        _ada_kernel,
        grid=(n // tn,),
        in_specs=[pl.BlockSpec((rows, D_MODEL), lambda j: (0, 0)),
                  pl.BlockSpec((D_MODEL, tn), lambda j: (0, j)),
                  pl.BlockSpec((1, tn), lambda j: (0, j))],
        out_specs=pl.BlockSpec((rows, tn), lambda j: (0, j)),
        out_shape=jax.ShapeDtypeStruct((rows, n), F32),
        compiler_params=_params("arbitrary"),
        name="ada_mod",
    )(c_all, w_ada, b_ada)


IN_TN = 512


def _inproj_kernel(x_ref, mod_ref, g_ref, w_ref, q_ref, k_ref, v_ref, kb_ref, vb_ref, p_ref, h_scr):
    j = pl.program_id(1)

    @pl.when(j == 0)
    def _():
        h = _rms(x_ref[...], g_ref[...], RMS_EPS) * (1.0 + mod_ref[0, 1]) + mod_ref[0, 0]
        h_scr[...] = h.astype(BF16)

    acc = _dot(h_scr[...], w_ref[...])

    @pl.when(j < 2)
    def _():
        q_ref[...] = (acc * ATT_SCALE).astype(BF16)

    @pl.when((j >= 2) & (j < 4))
    def _():
        k_ref[...] = acc
        kb_ref[...] = acc.astype(BF16)

    @pl.when((j >= 4) & (j < 6))
    def _():
        v_ref[...] = acc
        vb_ref[...] = acc.astype(BF16)

    @pl.when(j >= 6)
    def _():
        p_ref[...] = acc


def _inproj(x, mod4, g_pre, w_in_b, tm, tiles_per_mod):
    m = x.shape[0]
    r = mod4.shape[2]
    nj = w_in_b.shape[1] // IN_TN
    npj = P_PAD // IN_TN
    return pl.pallas_call(
        _inproj_kernel,
        grid=(m // tm, nj),
        in_specs=[pl.BlockSpec((tm, D_MODEL), lambda i, j: (i, 0)),
                  pl.BlockSpec((1, 6, r, D_MODEL), lambda i, j: (i // tiles_per_mod, 0, 0, 0)),
                  pl.BlockSpec((1, D_MODEL), lambda i, j: (0, 0)),
                  pl.BlockSpec((D_MODEL, IN_TN), lambda i, j: (0, j))],
        out_specs=[pl.BlockSpec((tm, IN_TN), lambda i, j: (i, jnp.clip(j, 0, 1))),
                   pl.BlockSpec((tm, IN_TN), lambda i, j: (i, jnp.clip(j - 2, 0, 1))),
                   pl.BlockSpec((tm, IN_TN), lambda i, j: (i, jnp.clip(j - 4, 0, 1))),
                   pl.BlockSpec((tm, IN_TN), lambda i, j: (i, jnp.clip(j - 2, 0, 1))),
                   pl.BlockSpec((tm, IN_TN), lambda i, j: (i, jnp.clip(j - 4, 0, 1))),
                   pl.BlockSpec((tm, IN_TN), lambda i, j: (i, jnp.clip(j - 6, 0, npj - 1)))],
        out_shape=[jax.ShapeDtypeStruct((m, ATT_WIDTH), BF16),
                   jax.ShapeDtypeStruct((m, ATT_WIDTH), F32),
                   jax.ShapeDtypeStruct((m, ATT_WIDTH), F32),
                   jax.ShapeDtypeStruct((m, ATT_WIDTH), BF16),
                   jax.ShapeDtypeStruct((m, ATT_WIDTH), BF16),
                   jax.ShapeDtypeStruct((m, P_PAD), F32)],
        scratch_shapes=[pltpu.VMEM((tm, D_MODEL), BF16)],
        compiler_params=_params("arbitrary", "arbitrary"),
        name="in_proj",
    )(x, mod4, g_pre, w_in_b)


ATT_T = 512
ATT_SUB = 128


def _pattn_kernel(qi_ref, ki_ref, tbl_ref, lamv_ref, bkt_ref, subg_ref, q_ref, k_ref, v_ref, o_ref,
                  bias_scr, q2_scr, m_scr, l_scr, acc_scr):
    h = pl.program_id(1)
    step = pl.program_id(2)
    qi = qi_ref[step]
    ki = ki_ref[step]
    nsub = ATT_T // ATT_SUB

    @pl.when(step == 0)
    def _():
        row = lax.broadcasted_iota(jnp.int32, (ATT_SUB, ATT_SUB), 0)
        col = lax.broadcasted_iota(jnp.int32, (ATT_SUB, ATT_SUB), 1)
        t0 = jnp.where(col > row, NEG_INF, _table_lookup(bkt_ref[0], tbl_ref, h))
        t1 = _table_lookup(bkt_ref[1], tbl_ref, h)
        far = jnp.full((ATT_SUB, ATT_SUB), tbl_ref[N_BUCKETS - 1, h], F32)
        masked = jnp.full((ATT_SUB, ATT_SUB), NEG_INF, F32)
        for rb in range(nsub):
            for cb in range(nsub):
                d = rb - cb
                diag = t0 if d == 0 else t1 if d == 1 else far if d >= 2 else masked
                off = t1 if (rb == 0 and cb == nsub - 1) else far
                rs, cs = slice(rb * ATT_SUB, (rb + 1) * ATT_SUB), slice(cb * ATT_SUB, (cb + 1) * ATT_SUB)
                bias_scr[0, rs, cs] = diag
                bias_scr[1, rs, cs] = off
                bias_scr[2, rs, cs] = far

    @pl.when(ki == 0)
    def _():
        q = q_ref[0]
        lane = lax.broadcasted_iota(jnp.int32, q.shape, 1)
        zero = jnp.zeros_like(q)
        q2_scr[0:ATT_T, :] = jnp.where(lane < HEAD_QK, q, zero)
        q2_scr[ATT_T:, :] = jnp.where(lane >= HEAD_QK, q, zero)
        m_scr[...] = jnp.full(m_scr.shape, NEG_INF, F32)
        l_scr[...] = jnp.zeros(l_scr.shape, F32)
        acc_scr[...] = jnp.zeros(acc_scr.shape, F32)

    bias = bias_scr[jnp.minimum(qi - ki, 2)]
    s = _dot_nt(q2_scr[...], k_ref[0]) + jnp.concatenate([bias, bias], axis=0)
    m_prev = m_scr[...]
    m_new = jnp.maximum(m_prev, jnp.max(s, axis=-1, keepdims=True))
    alpha = jnp.exp(m_prev - m_new)
    p = jnp.exp(s - _lanes(m_new, ATT_T // 128))
    l_scr[...] = alpha * l_scr[...] + jnp.sum(p, axis=-1, keepdims=True)
    acc_scr[...] = alpha * acc_scr[...] + _dot(p.astype(BF16), v_ref[0])
    m_scr[...] = m_new

    @pl.when(ki == qi)
    def _():
        on = acc_scr[...] / l_scr[...]
        o = on[0:ATT_T] - _lam(lamv_ref) * on[ATT_T:]
        o = _rms(o, subg_ref[...], SUBLN_EPS) * (1.0 - LAMBDA_INIT)
        o_ref[0] = o.astype(BF16)


def _prompt_attn(q, k, v, tbl, lamv, subg):
    b, s, _ = q.shape
    nq = s // ATT_T
    r = np.arange(ATT_SUB)
    dist = r[:, None] - r[None, :]
    bkt = jnp.asarray(np.stack([_t5_bucket_np(dist), _t5_bucket_np(dist + ATT_SUB)]))
    pairs = [(qi, ki) for qi in range(nq) for ki in range(qi + 1)]
    qi_of = jnp.asarray(np.array([p[0] for p in pairs], np.int32))
    ki_of = jnp.asarray(np.array([p[1] for p in pairs], np.int32))
    q_spec = pl.BlockSpec((1, ATT_T, HEAD_V), lambda bi, h, st, qi, ki: (bi, qi[st], h))
    kv_spec = pl.BlockSpec((1, ATT_T, HEAD_V), lambda bi, h, st, qi, ki: (bi, ki[st], h))
    const = lambda shape: pl.BlockSpec(shape, lambda bi, h, st, qi, ki: (0,) * len(shape))
    grid_spec = pltpu.PrefetchScalarGridSpec(
        num_scalar_prefetch=2,
        grid=(b, ATT_HEADS, len(pairs)),
        in_specs=[pl.BlockSpec(memory_space=pltpu.SMEM), const((4, HEAD_QK)), const((2, ATT_SUB, ATT_SUB)),
                  const((1, HEAD_V)), q_spec, kv_spec, kv_spec],
        out_specs=q_spec,
        scratch_shapes=[pltpu.VMEM((3, ATT_T, ATT_T), F32),
                        pltpu.VMEM((2 * ATT_T, HEAD_V), BF16),
                        pltpu.VMEM((2 * ATT_T, 128), F32),
                        pltpu.VMEM((2 * ATT_T, 128), F32),
                        pltpu.VMEM((2 * ATT_T, HEAD_V), F32)],
    )
    return pl.pallas_call(
        _pattn_kernel,
        grid_spec=grid_spec,
        out_shape=jax.ShapeDtypeStruct((b, s, ATT_WIDTH), BF16),
        compiler_params=_params("arbitrary", "arbitrary", "arbitrary"),
        name="prompt_attn",
    )(qi_of, ki_of, tbl, lamv, bkt, subg, q, k, v)


PAGES_PER_STEP = 8
QROWS = 16
PAGE_ROWS = PAGE * ATT_HEADS


def _sattn_kernel(pt_ref, tbl_ref, lamv_ref, bkt_ref, subg_ref, q_ref, kn_ref, vn_ref, *rest):
    g_pages = PAGES_PER_STEP
    k_refs = rest[:g_pages]
    v_refs = rest[g_pages:2 * g_pages]
    o_ref, bias_scr, m_scr, l_scr, acc_scr = rest[2 * g_pages:]
    del pt_ref
    b = pl.program_id(0)
    j = pl.program_id(1)
    nj = pl.num_programs(1)
    n_new = kn_ref.shape[1]
    nrow = ATT_HEADS * QROWS

    @pl.when((b == 0) & (j == 0))
    def _():
        key_head = lax.broadcasted_iota(jnp.int32, (QROWS, PAGE_ROWS), 1) % ATT_HEADS
        for h in range(ATT_HEADS):
            rows = slice(h * QROWS, (h + 1) * QROWS)
            bias_scr[0, rows, :] = jnp.where(key_head == h, tbl_ref[N_BUCKETS - 1, h], NEG_INF)
            bias_scr[1, rows, :] = jnp.where(key_head == h, _table_lookup(bkt_ref[...], tbl_ref, h), NEG_INF)

    @pl.when(j == 0)
    def _():
        m_scr[...] = jnp.full(m_scr.shape, NEG_INF, F32)
        l_scr[...] = jnp.zeros(l_scr.shape, F32)
        acc_scr[...] = jnp.zeros(acc_scr.shape, F32)

    q = q_ref[0]
    m_run, l_run, acc = m_scr[...], l_scr[...], acc_scr[...]
    for g in range(g_pages):
        bias = bias_scr[(j == nj - 1).astype(jnp.int32)] if g == g_pages - 1 else bias_scr[0]
        s = _dot_nt(q, k_refs[g][0].astype(BF16)) + bias
        m_new = jnp.maximum(m_run, jnp.max(s, axis=-1, keepdims=True))
        alpha = jnp.exp(m_run - m_new)
        p = jnp.exp(s - _lanes(m_new, PAGE_ROWS // 128))
        l_run = alpha * l_run + jnp.sum(p, axis=-1, keepdims=True)
        acc = alpha * acc + _dot(p.astype(BF16), v_refs[g][0].astype(BF16))
        m_run = m_new
    m_scr[...] = m_run
    l_scr[...] = l_run
    acc_scr[...] = acc

    @pl.when(j == nj - 1)
    def _():
        lam = _lam(lamv_ref)
        qf = q.astype(F32)
        t_row = lax.broadcasted_iota(jnp.int32, (nrow, 1), 0) % 8
        head_of_row = lax.broadcasted_iota(jnp.int32, (nrow, 1), 0) // QROWS
        m_fin, l_fin, acc_fin = m_run, l_run, acc
        tbl_rows = []
        for dd in range(n_new):
            tbl_row = jnp.zeros((nrow, 1), F32)
            for h in range(ATT_HEADS):
                tbl_row = jnp.where(head_of_row == h, tbl_ref[dd, h], tbl_row)
            tbl_rows.append(tbl_row)
        for tn in range(n_new):
            d = t_row - tn
            bias = jnp.full((nrow, 1), NEG_INF, F32)
            for dd in range(n_new):
                bias = jnp.where(d == dd, tbl_rows[dd], bias)
            k_rows = jnp.concatenate(
                [jnp.broadcast_to(kn_ref[0, tn:tn + 1, h * HEAD_V:(h + 1) * HEAD_V], (QROWS, HEAD_V))
                 for h in range(ATT_HEADS)], axis=0)
            v_rows = jnp.concatenate(
                [jnp.broadcast_to(vn_ref[0, tn:tn + 1, h * HEAD_V:(h + 1) * HEAD_V], (QROWS, HEAD_V))
                 for h in range(ATT_HEADS)], axis=0)
            s = jnp.sum(qf * k_rows, axis=-1, keepdims=True) + bias
            m_new = jnp.maximum(m_fin, s)
            alpha = jnp.exp(m_fin - m_new)
            p = jnp.exp(s - m_new)
            l_fin = alpha * l_fin + p
            acc_fin = alpha * acc_fin + p * v_rows
            m_fin = m_new
        on = acc_fin / l_fin
        for h in range(ATT_HEADS):
            o = on[h * QROWS:h * QROWS + 8] - lam * on[h * QROWS + 8:(h + 1) * QROWS]
            o = _rms(o, subg_ref[...], SUBLN_EPS) * (1.0 - LAMBDA_INIT)
            o_ref[0, :, h * HEAD_V:(h + 1) * HEAD_V] = o[0:n_new]


def _sample_attn(q_rows, k_new, v_new, cache_k, cache_v, page_table, tbl, lamv, subg):
    db, t_new = k_new.shape[0], k_new.shape[1]
    n_pages = page_table.shape[1]
    g_pages = PAGES_PER_STEP
    nrow = ATT_HEADS * QROWS
    t_row = (np.arange(QROWS) % 8)[:, None]
    tok = (np.arange(PAGE_ROWS) // ATT_HEADS)[None, :]
    bkt = jnp.asarray(_t5_bucket_np(PAGE + t_row - tok))

    def page_spec(g):
        return pl.BlockSpec((1, PAGE_ROWS, HEAD_V), lambda b, j, pt: (pt[b, j * g_pages + g], 0, 0))

    new_spec = pl.BlockSpec((1, t_new, ATT_WIDTH), lambda b, j, pt: (b, 0, 0))
    grid_spec = pltpu.PrefetchScalarGridSpec(
        num_scalar_prefetch=1,
        grid=(db, n_pages // g_pages),
        in_specs=[pl.BlockSpec(memory_space=pltpu.SMEM),
                  pl.BlockSpec((4, HEAD_QK), lambda b, j, pt: (0, 0)),
                  pl.BlockSpec((QROWS, PAGE_ROWS), lambda b, j, pt: (0, 0)),
                  pl.BlockSpec((1, HEAD_V), lambda b, j, pt: (0, 0)),
                  pl.BlockSpec((1, nrow, HEAD_V), lambda b, j, pt: (b, 0, 0)),
                  new_spec, new_spec]
                 + [page_spec(g) for g in range(g_pages)]
                 + [page_spec(g) for g in range(g_pages)],
        out_specs=new_spec,
        scratch_shapes=[pltpu.VMEM((2, nrow, PAGE_ROWS), F32),
                        pltpu.VMEM((nrow, 128), F32),
                        pltpu.VMEM((nrow, 128), F32),
                        pltpu.VMEM((nrow, HEAD_V), F32)],
    )
    return pl.pallas_call(
        _sattn_kernel,
        grid_spec=grid_spec,
        out_shape=jax.ShapeDtypeStruct((db, t_new, ATT_WIDTH), F32),
        compiler_params=_params("arbitrary", "arbitrary"),
        name="sample_attn",
    )(page_table, tbl, lamv, bkt, subg, q_rows, k_new, v_new,
      *([cache_k] * g_pages), *([cache_v] * g_pages))


def _prep_math(p, ps, mu, w0, a0, kkw, kaw, rkw, wwa, wg, ones_bd, out_refs):
    pm = p + mu * (ps - p)
    r = pm[:, 0:RWKV_WIDTH]
    kr = pm[:, RWKV_WIDTH:2 * RWKV_WIDTH]
    v = pm[:, 2 * RWKV_WIDTH:3 * RWKV_WIDTH]
    wa = pm[:, LORA_IN:LORA_IN + 128]
    lane = lax.broadcasted_iota(jnp.int32, wa.shape, 1)
    la = _dot(jnp.where(lane < 64, jnp.tanh(wa), wa).astype(BF16), wwa)
    z = -(w0 + la[:, :RWKV_WIDTH])
    softplus = jnp.maximum(z, 0.0) + jnp.log(1.0 + jnp.exp(-jnp.abs(z)))
    log_decay = -jnp.exp(-softplus - 0.5)
    asig = jax.nn.sigmoid(a0 + la[:, RWKV_WIDTH:])
    g = _dot(jax.nn.sigmoid(pm[:, GATE_IN:P_PAD]).astype(BF16), wg)
    kk = kr * kkw
    norm = jnp.sqrt(_dot((kk * kk).astype(BF16), ones_bd))
    kk = kk / jnp.maximum(norm, 1e-12)
    k2 = kr * (1.0 + (asig - 1.0) * kaw)
    bonus = _dot((r * k2 * rkw).astype(BF16), ones_bd) * v
    r_ref, lw_ref, k_ref, v_ref, a_ref, b_ref, g_ref, bonus_ref = out_refs
    r_ref[...] = r
    lw_ref[...] = log_decay
    k_ref[...] = k2
    v_ref[...] = v
    a_ref[...] = -kk
    b_ref[...] = kk * asig
    g_ref[...] = g
    bonus_ref[...] = bonus


def _prep_carry_kernel(p_ref, mu, w0, a0, kkw, kaw, rkw, wwa, wg, ones_bd, *rest):
    out_refs, carry = rest[:8], rest[8]
    i = pl.program_id(1)

    @pl.when(i == 0)
    def _():
        carry[...] = jnp.zeros(carry.shape, F32)

    p = p_ref[...]
    row = lax.broadcasted_iota(jnp.int32, p.shape, 0)
    ps = jnp.where(row == 0, carry[...], pltpu.roll(p, 1, 0))
    carry[...] = p[p.shape[0] - 1:, :]
    _prep_math(p, ps, mu[...], w0[...], a0[...], kkw[...], kaw[...], rkw[...], wwa[...], wg[...],
               ones_bd[...], out_refs)


def _prep_shift_kernel(p_ref, ps_ref, mu, w0, a0, kkw, kaw, rkw, wwa, wg, ones_bd, *out_refs):
    _prep_math(p_ref[...], ps_ref[...], mu[...], w0[...], a0[...], kkw[...], kaw[...], rkw[...], wwa[...],
               wg[...], ones_bd[...], out_refs)


def _rwkv_prep(p, p_shift, consts, n_seq, tm):
    m = p.shape[0]
    per_seq = m // n_seq // tm if p_shift is None else 0
    if p_shift is None:
        grid = (n_seq, per_seq)
        tile = lambda w: pl.BlockSpec((tm, w), lambda s, i: (s * per_seq + i, 0))
        const = lambda a: pl.BlockSpec(a.shape, lambda s, i: (0,) * a.ndim)
        kern, args, sem = _prep_carry_kernel, (p,), ("arbitrary", "arbitrary")
        scratch = [pltpu.VMEM((1, P_PAD), F32)]
    else:
        grid = (m // tm,)
        tile = lambda w: pl.BlockSpec((tm, w), lambda i: (i, 0))
        const = lambda a: pl.BlockSpec(a.shape, lambda i: (0,) * a.ndim)
        kern, args, sem = _prep_shift_kernel, (p, p_shift), ("arbitrary",)
        scratch = []
    return pl.pallas_call(
        kern,
        grid=grid,
        in_specs=[tile(P_PAD)] * len(args) + [const(a) for a in consts],
        out_specs=[tile(RWKV_WIDTH)] * 8,
        out_shape=[jax.ShapeDtypeStruct((m, RWKV_WIDTH), F32)] * 8,
        scratch_shapes=scratch,
        compiler_params=_params(*sem),
        name="rwkv_prep",
    )(*args, *consts)


def _wkv_kernel(r_ref, lw_ref, k_ref, v_ref, a_ref, b_ref, s0_ref, y_ref, sout_ref, s_scr):
    c = pl.program_id(1)
    nc = pl.num_programs(1)
    C = CHUNK
    pairs = range(RWKV_PAIRS)

    @pl.when(c == 0)
    def _():
        s_scr[...] = s0_ref[0]

    lw = lw_ref[0]
    trow = lax.broadcasted_iota(jnp.int32, (C, C), 0)
    tcol = lax.broadcasted_iota(jnp.int32, (C, C), 1)
    tri = jnp.where(tcol <= trow, 1.0, 0.0).astype(BF16)
    h1 = lw.astype(BF16)
    r1 = lw - h1.astype(F32)
    h2 = r1.astype(BF16)
    h3 = (r1 - h2.astype(F32)).astype(BF16)
    cs = _dot(tri, jnp.concatenate([h1, h2, h3], axis=1))
    cum = cs[:, :RWKV_WIDTH] + cs[:, RWKV_WIDTH:2 * RWKV_WIDTH] + cs[:, 2 * RWKV_WIDTH:]
    tot = cum[C - 1:C, :]
    e_inv = jnp.exp(-cum)
    e_tail = jnp.exp(tot - cum)
    at_all = a_ref[0] * jnp.exp(cum - lw)
    rt_all = r_ref[0] * jnp.exp(cum)
    bt_all = b_ref[0] * e_inv
    kt_all = k_ref[0] * e_inv
    bh_all = b_ref[0] * e_tail
    kh_all = k_ref[0] * e_tail
    w_tot = jnp.exp(tot)
    v_all = v_ref[0]

    row = lax.broadcasted_iota(jnp.int32, (2 * C, 2 * C), 0)
    col = lax.broadcasted_iota(jnp.int32, (2 * C, 2 * C), 1)
    same = (row // C) == (col // C)
    strict = same & (col < row)
    incl = same & (col <= row)
    eye = jnp.where(row == col, 1.0, 0.0)
    lane = lax.broadcasted_iota(jnp.int32, (C, 128), 1)
    first = lane < RWKV_HEAD
    bf = lambda x: x.astype(BF16)
    ls = [slice(pi * 128, (pi + 1) * 128) for pi in pairs]

    def stack(x):
        return jnp.concatenate([jnp.where(first, x, 0.0), jnp.where(first, 0.0, x)], axis=0)

    def dup(x):
        return jnp.concatenate([x, x], axis=0)

    at_s = [stack(at_all[:, s]) for s in ls]
    rt_s = [stack(rt_all[:, s]) for s in ls]
    v_s = [stack(v_all[:, s]) for s in ls]
    g = [_dot_nt(bf(jnp.concatenate([at_s[i], rt_s[i]], axis=0)),
                 bf(jnp.concatenate([dup(bt_all[:, ls[i]]), dup(kt_all[:, ls[i]])], axis=0))) for i in pairs]
    l_ab = [jnp.where(strict, g[i][0:2 * C, 0:2 * C], 0.0) for i in pairs]
    a_ak = [jnp.where(strict, g[i][0:2 * C, 2 * C:4 * C], 0.0) for i in pairs]
    a_rb = [jnp.where(incl, g[i][2 * C:4 * C, 0:2 * C], 0.0) for i in pairs]
    a_rk = [jnp.where(incl, g[i][2 * C:4 * C, 2 * C:4 * C], 0.0) for i in pairs]
    x = [eye + l for l in l_ab]
    pw = l_ab
    for _ in range(int(math.log2(C)) - 1):
        pw = [_dot(bf(m), bf(m)) for m in pw]
        x = [x[i] + _dot(bf(x[i]), bf(pw[i])) for i in pairs]
    av = [_dot(bf(jnp.concatenate([a_ak[i], a_rk[i]], axis=0)), bf(v_s[i])) for i in pairs]
    tx = [_dot(bf(x[i]), bf(jnp.concatenate([at_s[i], av[i][0:2 * C]], axis=1))) for i in pairs]
    s_old = [s_scr[i] for i in pairs]
    az = [_dot_nt(bf(jnp.concatenate([tx[i][:, 0:128], rt_s[i]], axis=0)), bf(s_old[i])) for i in pairs]
    u = [az[i][0:2 * C] + tx[i][:, 128:256] for i in pairs]
    y = [az[i][2 * C:4 * C] + _dot(bf(a_rb[i]), bf(u[i])) + av[i][2 * C:4 * C] for i in pairs]
    for i in pairs:
        y_ref[0, :, ls[i]] = y[i][0:C] + y[i][C:2 * C]
    upd = [_dot_tn(bf(jnp.concatenate([u[i], v_s[i]], axis=0)),
                   bf(jnp.concatenate([stack(bh_all[:, ls[i]]), stack(kh_all[:, ls[i]])], axis=0))) for i in pairs]
    for i in pairs:
        s_scr[i] = s_old[i] * w_tot[:, ls[i]] + upd[i]

    @pl.when(c == nc - 1)
    def _():
        sout_ref[0] = s_scr[...]


def _wkv(r, lw, k, v, a, b, s0_bd):
    n_seq, t, _ = r.shape
    seq_spec = pl.BlockSpec((1, CHUNK, RWKV_WIDTH), lambda s, c: (s, c, 0))
    st_spec = pl.BlockSpec((1, RWKV_PAIRS, 128, 128), lambda s, c: (s, 0, 0, 0))
    return pl.pallas_call(
        _wkv_kernel,
        grid=(n_seq, t // CHUNK),
        in_specs=[seq_spec] * 6 + [st_spec],
        out_specs=[seq_spec, st_spec],
        out_shape=[jax.ShapeDtypeStruct((n_seq, t, RWKV_WIDTH), F32),
                   jax.ShapeDtypeStruct((n_seq, RWKV_PAIRS, 128, 128), F32)],
        scratch_shapes=[pltpu.VMEM((RWKV_PAIRS, 128, 128), F32)],
        compiler_params=_params("arbitrary", "arbitrary"),
        name="wkv_scan",
    )(r, lw, k, v, a, b, s0_bd)


def _outproj_kernel(att_ref, y_ref, g_ref, bonus_ref, lng_ref, lnb_ref, ones_ref, wo_ref, x_ref, mod_ref,
                    gpost_ref, o_ref):
    y = y_ref[...]
    ones_bd = ones_ref[...]
    inv_n = 1.0 / RWKV_HEAD
    yc = y - _dot(y.astype(BF16), ones_bd) * inv_n
    var = _dot((yc * yc).astype(BF16), ones_bd) * inv_n
    yn = yc * lax.rsqrt(var + LNX_EPS)
    rw = (yn * lng_ref[...] + lnb_ref[...] + bonus_ref[...]) * g_ref[...]
    mix = (_dot(att_ref[...].astype(BF16), wo_ref[0:ATT_WIDTH, :])
           + _dot(rw.astype(BF16), wo_ref[ATT_WIDTH:, :]))
    o_ref[...] = x_ref[...] + mod_ref[0, 2] * _rms(mix, gpost_ref[...], RMS_EPS)


def _outproj(att, y, g, bonus, lng, lnb, ones_bd, w_out_b, x, mod4, g_post, tm, tiles_per_mod):
    m = x.shape[0]
    r = mod4.shape[2]
    tile = lambda w: pl.BlockSpec((tm, w), lambda i: (i, 0))
    const = lambda a: pl.BlockSpec(a.shape, lambda i: (0,) * a.ndim)
    return pl.pallas_call(
        _outproj_kernel,
        grid=(m // tm,),
        in_specs=[tile(ATT_WIDTH)] * 4 + [const(lng), const(lnb), const(ones_bd), const(w_out_b),
                                          tile(D_MODEL),
                                          pl.BlockSpec((1, 6, r, D_MODEL), lambda i: (i // tiles_per_mod, 0, 0, 0)),
                                          const(g_post)],
        out_specs=tile(D_MODEL),
        out_shape=jax.ShapeDtypeStruct((m, D_MODEL), F32),
        compiler_params=_params("arbitrary"),
        name="out_proj",
    )(att, y, g, bonus, lng, lnb, ones_bd, w_out_b, x, mod4, g_post)


FFN_TF = 1024


def _ffn_kernel(x_ref, mod_ref, gpre_ref, gpost_ref, wu_ref, wd_ref, o_ref, h_scr, acc_scr):
    f = pl.program_id(1)

    @pl.when(f == 0)
    def _():
        h = _rms(x_ref[...], gpre_ref[...], RMS_EPS) * (1.0 + mod_ref[0, 4]) + mod_ref[0, 3]
        h_scr[...] = h.astype(BF16)
        acc_scr[...] = jnp.zeros(acc_scr.shape, F32)

    u = jnp.maximum(_dot(h_scr[...], wu_ref[...]), 0.0)
    acc_scr[...] += _dot((u * u).astype(BF16), wd_ref[...])

    @pl.when(f == pl.num_programs(1) - 1)
    def _():
        o_ref[...] = x_ref[...] + mod_ref[0, 5] * _rms(acc_scr[...], gpost_ref[...], RMS_EPS)


def _ffn(x, mod4, g_pre, g_post, w_up_b, w_down_b, tm, tiles_per_mod):
    m = x.shape[0]
    r = mod4.shape[2]
    return pl.pallas_call(
        _ffn_kernel,
        grid=(m // tm, FFN_DIM // FFN_TF),
        in_specs=[pl.BlockSpec((tm, D_MODEL), lambda i, f: (i, 0)),
                  pl.BlockSpec((1, 6, r, D_MODEL), lambda i, f: (i // tiles_per_mod, 0, 0, 0)),
                  pl.BlockSpec((1, D_MODEL), lambda i, f: (0, 0)),
                  pl.BlockSpec((1, D_MODEL), lambda i, f: (0, 0)),
                  pl.BlockSpec((D_MODEL, FFN_TF), lambda i, f: (0, f)),
                  pl.BlockSpec((FFN_TF, D_MODEL), lambda i, f: (f, 0))],
        out_specs=pl.BlockSpec((tm, D_MODEL), lambda i, f: (i, 0)),
        out_shape=jax.ShapeDtypeStruct((m, D_MODEL), F32),
        scratch_shapes=[pltpu.VMEM((tm, D_MODEL), BF16), pltpu.VMEM((tm, D_MODEL), F32)],
        compiler_params=_params("arbitrary", "arbitrary"),
        name="ffn",
    )(x, mod4, g_pre, g_post, w_up_b, w_down_b)


def _state_to_blockdiag(s):
    n = s.shape[0]
    s = s.reshape(n, RWKV_PAIRS, 2, RWKV_HEAD, RWKV_HEAD)
    z = jnp.zeros((n, RWKV_PAIRS, RWKV_HEAD, RWKV_HEAD), s.dtype)
    top = jnp.concatenate([s[:, :, 0], z], axis=-1)
    bot = jnp.concatenate([z, s[:, :, 1]], axis=-1)
    return jnp.concatenate([top, bot], axis=-2)


def _blockdiag_to_state(sbd):
    n = sbd.shape[0]
    s = jnp.stack([sbd[:, :, :RWKV_HEAD, :RWKV_HEAD], sbd[:, :, RWKV_HEAD:, RWKV_HEAD:]], axis=2)
    return s.reshape(n, 2 * RWKV_PAIRS, RWKV_HEAD, RWKV_HEAD)


def _pad_cols(a, width):
    return jnp.pad(a, ((0, 0), (0, width - a.shape[1])))


def kernel(x_prompt, x_sample, cache_k, cache_v, state_wkv, state_shift, page_table, c_prompt, c_sample, bias_table, w_ada, b_ada, g_pre_mix, g_post_mix, g_pre_ffn, g_post_ffn, w_in, mu_shift, w0, w_lora_w, a0, w_lora_a, w_lora_g, k_k, k_a, r_k, lnx_g, lnx_b, lam_q1, lam_k1, lam_q2, lam_k2, subln_g, w_out, w_ffn_up, w_ffn_down):
    bsz, seq, d = x_prompt.shape
    dbs, tdec, _ = x_sample.shape
    n_p, n_s = bsz * seq, dbs * tdec

    w_in_b = jnp.concatenate([w_in[0], jnp.zeros((d, P_PAD - RWKV_PROJ), F32)], axis=1).astype(BF16)
    w_out_b = w_out[0].astype(BF16)
    w_up_b = w_ffn_up[0].astype(BF16)
    w_down_b = w_ffn_down[0].astype(BF16)
    wwa = jnp.zeros((128, 2 * RWKV_WIDTH), F32)
    wwa = wwa.at[:64, :RWKV_WIDTH].set(w_lora_w[0]).at[64:, RWKV_WIDTH:].set(w_lora_a[0]).astype(BF16)
    wg = jnp.pad(w_lora_g[0], ((0, P_PAD - GATE_IN - w_lora_g.shape[1]), (0, 0))).astype(BF16)
    head_id = jnp.arange(RWKV_WIDTH) // RWKV_HEAD
    ones_bd = (head_id[:, None] == head_id[None, :]).astype(BF16)
    prep_consts = (_pad_cols(mu_shift, P_PAD), w0, a0, k_k, k_a, r_k.reshape(1, RWKV_WIDTH), wwa, wg, ones_bd)
    lamv = jnp.concatenate([lam_q1, lam_k1, lam_q2, lam_k2], axis=0)

    n_c = bsz + dbs
    c_all = jnp.pad(jnp.concatenate([c_prompt, c_sample], axis=0), ((0, (-n_c) % 8), (0, 0)))
    mod = _ada(c_all, w_ada[0], b_ada)
    mod_p = mod[:bsz].reshape(bsz, 6, 1, d)
    mod_s = jnp.repeat(mod[bsz:n_c].reshape(dbs, 6, d), tdec, axis=0).transpose(1, 0, 2).reshape(1, 6, n_s, d)

    xp = x_prompt.reshape(n_p, d)
    xs = x_sample.reshape(n_s, d)
    tm_p = min(1024, seq)
    q_p, k_p, v_p, kb_p, vb_p, p_p = _inproj(xp, mod_p, g_pre_mix, w_in_b, tm_p, seq // tm_p)
    q_s, k_s, v_s, _, _, p_s = _inproj(xs, mod_s, g_pre_mix, w_in_b, n_s, 1)

    att_p = _prompt_attn(q_p.reshape(bsz, seq, ATT_WIDTH), kb_p.reshape(bsz, seq, ATT_WIDTH),
                         vb_p.reshape(bsz, seq, ATT_WIDTH), bias_table, lamv, subln_g)
    q4 = q_s.reshape(dbs, tdec, ATT_HEADS, HEAD_V).transpose(0, 2, 1, 3)
    first = jnp.arange(HEAD_V) < HEAD_QK
    zq = jnp.zeros((dbs, ATT_HEADS, 8 - tdec, HEAD_V), BF16)
    q_rows = jnp.concatenate([jnp.where(first, q4, 0), zq, jnp.where(first, 0, q4), zq], axis=2)
    n_pool = cache_k.shape[1]
    att_s = _sample_attn(q_rows.reshape(dbs, ATT_HEADS * QROWS, HEAD_V),
                         k_s.reshape(dbs, tdec, ATT_WIDTH), v_s.reshape(dbs, tdec, ATT_WIDTH),
                         cache_k[0].reshape(n_pool, PAGE_ROWS, HEAD_V), cache_v[0].reshape(n_pool, PAGE_ROWS, HEAD_V),
                         page_table, bias_table, lamv, subln_g)

    pre_p = _rwkv_prep(p_p, None, prep_consts, bsz, min(256, seq))
    p_s3 = p_s.reshape(dbs, tdec, P_PAD)
    shift_s = jnp.concatenate([_pad_cols(state_shift[0], P_PAD)[:, None, :], p_s3[:, :-1]], axis=1)
    pre_s = _rwkv_prep(p_s, shift_s.reshape(n_s, P_PAD), prep_consts, dbs, n_s)
    r_p, lw_p, kk_p, vv_p, a_p, b_p, g_p, bonus_p = pre_p
    r_s, lw_s, kk_s, vv_s, a_s, b_s, g_s, bonus_s = pre_s
    seq3 = lambda t: t.reshape(bsz, seq, RWKV_WIDTH)
    y_p, sbd_p = _wkv(seq3(r_p), seq3(lw_p), seq3(kk_p), seq3(vv_p), seq3(a_p), seq3(b_p),
                      jnp.zeros((bsz, RWKV_PAIRS, 128, 128), F32))
    dec3 = lambda t: jnp.pad(t.reshape(dbs, tdec, RWKV_WIDTH), ((0, 0), (0, CHUNK - tdec), (0, 0)))
    y_s, sbd_s = _wkv(dec3(r_s), dec3(lw_s), dec3(kk_s), dec3(vv_s), dec3(a_s), dec3(b_s),
                      _state_to_blockdiag(state_wkv[0]))
    y_s = y_s[:, :tdec].reshape(n_s, RWKV_WIDTH)

    tm_o = min(256, seq)
    x1_p = _outproj(att_p.reshape(n_p, ATT_WIDTH), y_p.reshape(n_p, RWKV_WIDTH), g_p, bonus_p, lnx_g, lnx_b,
                    ones_bd, w_out_b, xp, mod_p, g_post_mix, tm_o, seq // tm_o)
    x1_s = _outproj(att_s.reshape(n_s, ATT_WIDTH), y_s, g_s, bonus_s, lnx_g, lnx_b,
                    ones_bd, w_out_b, xs, mod_s, g_post_mix, n_s, 1)
    tm_f = min(512, seq)
    out_p = _ffn(x1_p, mod_p, g_pre_ffn, g_post_ffn, w_up_b, w_down_b, tm_f, seq // tm_f)
    out_s = _ffn(x1_s, mod_s, g_pre_ffn, g_post_ffn, w_up_b, w_down_b, n_s, 1)

    return (out_p.reshape(bsz, seq, d),
            out_s.reshape(dbs, tdec, d),
            k_p.reshape(1, bsz, seq, ATT_HEADS, HEAD_V),
            v_p.reshape(1, bsz, seq, ATT_HEADS, HEAD_V),
            _blockdiag_to_state(sbd_p)[None],
            p_p.reshape(bsz, seq, P_PAD)[None, :, -1, :RWKV_PROJ],
            k_s.reshape(1, dbs, tdec, ATT_HEADS, HEAD_V),
            v_s.reshape(1, dbs, tdec, ATT_HEADS, HEAD_V),
            _blockdiag_to_state(sbd_s)[None],
            p_s3[None, :, -1, :RWKV_PROJ])
```

```python
import functools
import math

import numpy as np
import jax
import jax.numpy as jnp
from jax import lax
from jax.experimental import pallas as pl
from jax.experimental.pallas import tpu as pltpu

F32 = jnp.float32
BF16 = jnp.bfloat16

D_MODEL = 2048
ATT_WIDTH = 1024
RWKV_WIDTH = 1024
ATT_HEADS = 8
HEAD_V = 128
HEAD_QK = 64
RWKV_HEAD = 64
RWKV_PAIRS = RWKV_WIDTH // 128
RWKV_PROJ = 3360
P_PAD = 3584
LORA_IN = 3072
GATE_IN = 3200
FFN_DIM = 8192
N_BUCKETS = 32
MAX_DISTANCE = 128
PAGE = 128
ATT_SCALE = HEAD_QK ** -0.5
RMS_EPS = 1e-6
SUBLN_EPS = 1e-5
LNX_EPS = 64e-5
NEG_INF = -1e30
LAMBDA_INIT = 0.8 - 0.6 * math.exp(-0.3 * 0)
CHUNK = 64
VMEM_LIMIT = 56 * 1024 * 1024


def _params(*sem):
    return pltpu.CompilerParams(dimension_semantics=sem, vmem_limit_bytes=VMEM_LIMIT)


def _dot(a, b):
    return jnp.dot(a, b, preferred_element_type=F32)


def _dot_nt(a, b):
    return lax.dot_general(a, b, (((1,), (1,)), ((), ())), preferred_element_type=F32)


def _dot_tn(a, b):
    return lax.dot_general(a, b, (((0,), (0,)), ((), ())), preferred_element_type=F32)


def _t5_bucket_np(dist):
    max_exact = N_BUCKETS // 2
    d = np.maximum(dist, 0)
    ratio = np.log(np.maximum(d, 1).astype(np.float32) / max_exact) / math.log(MAX_DISTANCE / max_exact)
    large = np.minimum(max_exact + (ratio * (N_BUCKETS - max_exact)).astype(np.int32), N_BUCKETS - 1)
    return np.where(d < max_exact, d, large).astype(np.int32)


def _table_lookup(bucket, tbl_ref, h):
    out = jnp.zeros(bucket.shape, F32)
    for b in range(N_BUCKETS):
        out = jnp.where(bucket == b, tbl_ref[b, h], out)
    return out


def _lam(lamv_ref):
    v = lamv_ref[...]
    s1 = jnp.sum(v[0:1] * v[1:2], axis=-1, keepdims=True)
    s2 = jnp.sum(v[2:3] * v[3:4], axis=-1, keepdims=True)
    return jnp.exp(s1) - jnp.exp(s2) + LAMBDA_INIT


def _rms(x, g, eps):
    return x * lax.rsqrt(jnp.mean(x * x, axis=-1, keepdims=True) + eps) * g


def _lanes(x, reps):
    return jnp.concatenate([x] * reps, axis=1)


def _ada_kernel(c_ref, w_ref, b_ref, o_ref):
    c = c_ref[...]
    x = (c * jax.nn.sigmoid(c)).astype(BF16)
    o_ref[...] = _dot(x, w_ref[...].astype(BF16)) + b_ref[...]


def _ada(c_all, w_ada, b_ada):
    rows, n = c_all.shape[0], w_ada.shape[1]
    tn = 1536
    return pl.pallas_call(
        _ada_kernel,
        grid=(n // tn,),
        in_specs=[pl.BlockSpec((rows, D_MODEL), lambda j: (0, 0)),
                  pl.BlockSpec((D_MODEL, tn), lambda j: (0, j)),
                  pl.BlockSpec((1, tn), lambda j: (0, j))],
        out_specs=pl.BlockSpec((rows, tn), lambda j: (0, j)),
        out_shape=jax.ShapeDtypeStruct((rows, n), F32),
        compiler_params=_params("arbitrary"),
        name="ada_mod",
    )(c_all, w_ada, b_ada)


IN_TN = 512


def _inproj_kernel(x_ref, mod_ref, g_ref, w_ref, q_ref, k_ref, v_ref, kb_ref, vb_ref, p_ref, h_scr):
    j = pl.program_id(1)

    @pl.when(j == 0)
    def _():
        h = _rms(x_ref[...], g_ref[...], RMS_EPS) * (1.0 + mod_ref[0, 1]) + mod_ref[0, 0]
        h_scr[...] = h.astype(BF16)

    acc = _dot(h_scr[...], w_ref[...])

    @pl.when(j < 2)
    def _():
        q_ref[...] = (acc * ATT_SCALE).astype(BF16)

    @pl.when((j >= 2) & (j < 4))
    def _():
        k_ref[...] = acc
        kb_ref[...] = acc.astype(BF16)

    @pl.when((j >= 4) & (j < 6))
    def _():
        v_ref[...] = acc
        vb_ref[...] = acc.astype(BF16)

    @pl.when(j >= 6)
    def _():
        p_ref[...] = acc


def _inproj(x, mod4, g_pre, w_in_b, tm, tiles_per_mod):
    m = x.shape[0]
    r = mod4.shape[2]
    nj = w_in_b.shape[1] // IN_TN
    npj = P_PAD // IN_TN
    return pl.pallas_call(
        _inproj_kernel,
        grid=(m // tm, nj),
        in_specs=[pl.BlockSpec((tm, D_MODEL), lambda i, j: (i, 0)),
                  pl.BlockSpec((1, 6, r, D_MODEL), lambda i, j: (i // tiles_per_mod, 0, 0, 0)),
                  pl.BlockSpec((1, D_MODEL), lambda i, j: (0, 0)),
                  pl.BlockSpec((D_MODEL, IN_TN), lambda i, j: (0, j))],
        out_specs=[pl.BlockSpec((tm, IN_TN), lambda i, j: (i, jnp.clip(j, 0, 1))),
                   pl.BlockSpec((tm, IN_TN), lambda i, j: (i, jnp.clip(j - 2, 0, 1))),
                   pl.BlockSpec((tm, IN_TN), lambda i, j: (i, jnp.clip(j - 4, 0, 1))),
                   pl.BlockSpec((tm, IN_TN), lambda i, j: (i, jnp.clip(j - 2, 0, 1))),
                   pl.BlockSpec((tm, IN_TN), lambda i, j: (i, jnp.clip(j - 4, 0, 1))),
                   pl.BlockSpec((tm, IN_TN), lambda i, j: (i, jnp.clip(j - 6, 0, npj - 1)))],
        out_shape=[jax.ShapeDtypeStruct((m, ATT_WIDTH), BF16),
                   jax.ShapeDtypeStruct((m, ATT_WIDTH), F32),
                   jax.ShapeDtypeStruct((m, ATT_WIDTH), F32),
                   jax.ShapeDtypeStruct((m, ATT_WIDTH), BF16),
                   jax.ShapeDtypeStruct((m, ATT_WIDTH), BF16),
                   jax.ShapeDtypeStruct((m, P_PAD), F32)],
        scratch_shapes=[pltpu.VMEM((tm, D_MODEL), BF16)],
        compiler_params=_params("arbitrary", "arbitrary"),
        name="in_proj",
    )(x, mod4, g_pre, w_in_b)


ATT_T = 512
ATT_SUB = 128


def _pattn_kernel(qi_ref, ki_ref, tbl_ref, lamv_ref, bkt_ref, subg_ref, q_ref, k_ref, v_ref, o_ref,
                  bias_scr, q2_scr, m_scr, l_scr, acc_scr):
    h = pl.program_id(1)
    step = pl.program_id(2)
    qi = qi_ref[step]
    ki = ki_ref[step]
    nsub = ATT_T // ATT_SUB

    @pl.when(step == 0)
    def _():
        row = lax.broadcasted_iota(jnp.int32, (ATT_SUB, ATT_SUB), 0)
        col = lax.broadcasted_iota(jnp.int32, (ATT_SUB, ATT_SUB), 1)
        t0 = jnp.where(col > row, NEG_INF, _table_lookup(bkt_ref[0], tbl_ref, h))
        t1 = _table_lookup(bkt_ref[1], tbl_ref, h)
        far = jnp.full((ATT_SUB, ATT_SUB), tbl_ref[N_BUCKETS - 1, h], F32)
        masked = jnp.full((ATT_SUB, ATT_SUB), NEG_INF, F32)
        for rb in range(nsub):
            for cb in range(nsub):
                d = rb - cb
                diag = t0 if d == 0 else t1 if d == 1 else far if d >= 2 else masked
                off = t1 if (rb == 0 and cb == nsub - 1) else far
                rs, cs = slice(rb * ATT_SUB, (rb + 1) * ATT_SUB), slice(cb * ATT_SUB, (cb + 1) * ATT_SUB)
                bias_scr[0, rs, cs] = diag
                bias_scr[1, rs, cs] = off
                bias_scr[2, rs, cs] = far

    @pl.when(ki == 0)
    def _():
        q = q_ref[0]
        lane = lax.broadcasted_iota(jnp.int32, q.shape, 1)
        zero = jnp.zeros_like(q)
        q2_scr[0:ATT_T, :] = jnp.where(lane < HEAD_QK, q, zero)
        q2_scr[ATT_T:, :] = jnp.where(lane >= HEAD_QK, q, zero)
        m_scr[...] = jnp.full(m_scr.shape, NEG_INF, F32)
        l_scr[...] = jnp.zeros(l_scr.shape, F32)
        acc_scr[...] = jnp.zeros(acc_scr.shape, F32)

    bias = bias_scr[jnp.minimum(qi - ki, 2)]
    s = _dot_nt(q2_scr[...], k_ref[0]) + jnp.concatenate([bias, bias], axis=0)
    m_prev = m_scr[...]
    m_new = jnp.maximum(m_prev, jnp.max(s, axis=-1, keepdims=True))
    alpha = jnp.exp(m_prev - m_new)
    p = jnp.exp(s - _lanes(m_new, ATT_T // 128))
    l_scr[...] = alpha * l_scr[...] + jnp.sum(p, axis=-1, keepdims=True)
    acc_scr[...] = alpha * acc_scr[...] + _dot(p.astype(BF16), v_ref[0])
    m_scr[...] = m_new

    @pl.when(ki == qi)
    def _():
        on = acc_scr[...] / l_scr[...]
        o = on[0:ATT_T] - _lam(lamv_ref) * on[ATT_T:]
        o = _rms(o, subg_ref[...], SUBLN_EPS) * (1.0 - LAMBDA_INIT)
        o_ref[0] = o.astype(BF16)


def _prompt_attn(q, k, v, tbl, lamv, subg):
    b, s, _ = q.shape
    nq = s // ATT_T
    r = np.arange(ATT_SUB)
    dist = r[:, None] - r[None, :]
    bkt = jnp.asarray(np.stack([_t5_bucket_np(dist), _t5_bucket_np(dist + ATT_SUB)]))
    pairs = [(qi, ki) for qi in range(nq) for ki in range(qi + 1)]
    qi_of = jnp.asarray(np.array([p[0] for p in pairs], np.int32))
    ki_of = jnp.asarray(np.array([p[1] for p in pairs], np.int32))
    q_spec = pl.BlockSpec((1, ATT_T, HEAD_V), lambda bi, h, st, qi, ki: (bi, qi[st], h))
    kv_spec = pl.BlockSpec((1, ATT_T, HEAD_V), lambda bi, h, st, qi, ki: (bi, ki[st], h))
    const = lambda shape: pl.BlockSpec(shape, lambda bi, h, st, qi, ki: (0,) * len(shape))
    grid_spec = pltpu.PrefetchScalarGridSpec(
        num_scalar_prefetch=2,
        grid=(b, ATT_HEADS, len(pairs)),
        in_specs=[pl.BlockSpec(memory_space=pltpu.SMEM), const((4, HEAD_QK)), const((2, ATT_SUB, ATT_SUB)),
                  const((1, HEAD_V)), q_spec, kv_spec, kv_spec],
        out_specs=q_spec,
        scratch_shapes=[pltpu.VMEM((3, ATT_T, ATT_T), F32),
                        pltpu.VMEM((2 * ATT_T, HEAD_V), BF16),
                        pltpu.VMEM((2 * ATT_T, 128), F32),
                        pltpu.VMEM((2 * ATT_T, 128), F32),
                        pltpu.VMEM((2 * ATT_T, HEAD_V), F32)],
    )
    return pl.pallas_call(
        _pattn_kernel,
        grid_spec=grid_spec,
        out_shape=jax.ShapeDtypeStruct((b, s, ATT_WIDTH), BF16),
        compiler_params=_params("arbitrary", "arbitrary", "arbitrary"),
        name="prompt_attn",
    )(qi_of, ki_of, tbl, lamv, bkt, subg, q, k, v)


PAGES_PER_STEP = 8
QROWS = 16
QCOLS = ATT_HEADS * QROWS
PAGE_ROWS = PAGE * ATT_HEADS


def _sattn_build_bias(bias_scr, bkt_ref, tbl_ref):
    lane_head = lax.broadcasted_iota(jnp.int32, (PAGE_ROWS, QCOLS), 1) // QROWS
    bkt = bkt_ref[...]
    delta = jnp.zeros((PAGE_ROWS, QCOLS), F32)
    for h in range(ATT_HEADS):
        delta = jnp.where(lane_head == h, _table_lookup(bkt, tbl_ref, h) - tbl_ref[N_BUCKETS - 1, h], delta)
    bias_scr[0] = jnp.zeros((PAGE_ROWS, QCOLS), F32)
    bias_scr[1] = delta


def _sattn_update(z, v_bf, m_prev, l_prev, acc_prev):
    m_new = jnp.maximum(m_prev, jnp.max(z, axis=0, keepdims=True))
    alpha = jnp.exp(m_prev - m_new)
    p = jnp.exp(z - m_new[0:1])
    l_new = alpha * l_prev + jnp.sum(p, axis=0, keepdims=True)
    acc_new = alpha[0:1] * acc_prev + _dot_tn(v_bf, p.astype(BF16))
    return m_new, l_new, acc_new


def _sattn_scores(qaug_t, kmask, k_refs, delta):
    kall = jnp.concatenate([jnp.concatenate([kr[0].astype(BF16), kmask], axis=1) for kr in k_refs], axis=0)
    z = _dot(kall, qaug_t)
    tail = z.shape[0] - PAGE_ROWS
    return jnp.concatenate([z[:tail], z[tail:] + delta], axis=0)


def _sattn_finish(qaug_t, kmask, kn_ref, vn_ref, tbl_ref, lamv_ref, subg_ref, o_ref, state):
    rows = kn_ref.shape[1]
    n_new = rows // ATT_HEADS
    row_t = lax.broadcasted_iota(jnp.int32, (rows, QCOLS), 0) // ATT_HEADS
    lane = lax.broadcasted_iota(jnp.int32, (rows, QCOLS), 1)
    d = lane % 8 - row_t
    lane_head = lane // QROWS
    delta = jnp.full((rows, QCOLS), NEG_INF, F32)
    for h in range(ATT_HEADS):
        for dd in range(n_new):
            delta = jnp.where((lane_head == h) & (d == dd), tbl_ref[dd, h] - tbl_ref[N_BUCKETS - 1, h], delta)
    z = _dot(jnp.concatenate([kn_ref[0].astype(BF16), kmask[0:rows]], axis=1), qaug_t) + delta
    _, l_fin, acc_fin = _sattn_update(z, vn_ref[0].astype(BF16), *state)
    on = (acc_fin / l_fin[0:1]).T
    lam = _lam(lamv_ref)
    for h in range(ATT_HEADS):
        o = on[h * QROWS:h * QROWS + 8] - lam * on[h * QROWS + 8:(h + 1) * QROWS]
        o = _rms(o, subg_ref[...], SUBLN_EPS) * (1.0 - LAMBDA_INIT)
        o_ref[0, :, h * HEAD_V:(h + 1) * HEAD_V] = o[0:n_new]


def _prep_math(p, ps, mu, w0, a0, kkw, kaw, rkw, wwa, wg, ones_bd, out_refs):
    pm = p + mu * (ps - p)
    r = pm[:, 0:RWKV_WIDTH]
    kr = pm[:, RWKV_WIDTH:2 * RWKV_WIDTH]
    v = pm[:, 2 * RWKV_WIDTH:3 * RWKV_WIDTH]
    wa = pm[:, LORA_IN:LORA_IN + 128]
    lane = lax.broadcasted_iota(jnp.int32, wa.shape, 1)
    la = _dot(jnp.where(lane < 64, jnp.tanh(wa), wa).astype(BF16), wwa)
    z = -(w0 + la[:, :RWKV_WIDTH])
    softplus = jnp.maximum(z, 0.0) + jnp.log(1.0 + jnp.exp(-jnp.abs(z)))
    log_decay = -jnp.exp(-softplus - 0.5)
    asig = jax.nn.sigmoid(a0 + la[:, RWKV_WIDTH:])
    g = _dot(jax.nn.sigmoid(pm[:, GATE_IN:P_PAD]).astype(BF16), wg)
    kk = kr * kkw
    norm = jnp.sqrt(_dot((kk * kk).astype(BF16), ones_bd))
    kk = kk / jnp.maximum(norm, 1e-12)
    k2 = kr * (1.0 + (asig - 1.0) * kaw)
    bonus = _dot((r * k2 * rkw).astype(BF16), ones_bd) * v
    r_ref, lw_ref, k_ref, v_ref, a_ref, b_ref, g_ref, bonus_ref = out_refs
    r_ref[...] = r
    lw_ref[...] = log_decay
    k_ref[...] = k2
    v_ref[...] = v
    a_ref[...] = -kk
    b_ref[...] = kk * asig
    g_ref[...] = g
    bonus_ref[...] = bonus


def _prep_carry_kernel(p_ref, mu, w0, a0, kkw, kaw, rkw, wwa, wg, ones_bd, *rest):
    out_refs, carry = rest[:8], rest[8]
    i = pl.program_id(1)

    @pl.when(i == 0)
    def _():
        carry[...] = jnp.zeros(carry.shape, F32)

    p = p_ref[...]
    row = lax.broadcasted_iota(jnp.int32, p.shape, 0)
    ps = jnp.where(row == 0, carry[...], pltpu.roll(p, 1, 0))
    carry[...] = p[p.shape[0] - 1:, :]
    _prep_math(p, ps, mu[...], w0[...], a0[...], kkw[...], kaw[...], rkw[...], wwa[...], wg[...],
               ones_bd[...], out_refs)


def _prep_shift_kernel(p_ref, ps_ref, mu, w0, a0, kkw, kaw, rkw, wwa, wg, ones_bd, *out_refs):
    _prep_math(p_ref[...], ps_ref[...], mu[...], w0[...], a0[...], kkw[...], kaw[...], rkw[...], wwa[...],
               wg[...], ones_bd[...], out_refs)


def _rwkv_prep(p, p_shift, consts, n_seq, tm):
    m = p.shape[0]
    per_seq = m // n_seq // tm if p_shift is None else 0
    if p_shift is None:
        grid = (n_seq, per_seq)
        tile = lambda w: pl.BlockSpec((tm, w), lambda s, i: (s * per_seq + i, 0))
        const = lambda a: pl.BlockSpec(a.shape, lambda s, i: (0,) * a.ndim)
        kern, args, sem = _prep_carry_kernel, (p,), ("arbitrary", "arbitrary")
        scratch = [pltpu.VMEM((1, P_PAD), F32)]
    else:
        grid = (m // tm,)
        tile = lambda w: pl.BlockSpec((tm, w), lambda i: (i, 0))
        const = lambda a: pl.BlockSpec(a.shape, lambda i: (0,) * a.ndim)
        kern, args, sem = _prep_shift_kernel, (p, p_shift), ("arbitrary",)
        scratch = []
    return pl.pallas_call(
        kern,
        grid=grid,
        in_specs=[tile(P_PAD)] * len(args) + [const(a) for a in consts],
        out_specs=[tile(RWKV_WIDTH)] * 8,
        out_shape=[jax.ShapeDtypeStruct((m, RWKV_WIDTH), F32)] * 8,
        scratch_shapes=scratch,
        compiler_params=_params(*sem),
        name="rwkv_prep",
    )(*args, *consts)


def _wkv_kernel(r_ref, lw_ref, k_ref, v_ref, a_ref, b_ref, s0_ref, y_ref, sout_ref, s_scr):
    c = pl.program_id(1)
    nc = pl.num_programs(1)
    C = CHUNK
    pairs = range(RWKV_PAIRS)

    @pl.when(c == 0)
    def _():
        s_scr[...] = s0_ref[0]

    lw = lw_ref[0]
    trow = lax.broadcasted_iota(jnp.int32, (C, C), 0)
    tcol = lax.broadcasted_iota(jnp.int32, (C, C), 1)
    tri = jnp.where(tcol <= trow, 1.0, 0.0).astype(BF16)
    h1 = lw.astype(BF16)
    r1 = lw - h1.astype(F32)
    h2 = r1.astype(BF16)
    h3 = (r1 - h2.astype(F32)).astype(BF16)
    cs = _dot(tri, jnp.concatenate([h1, h2, h3], axis=1))
    cum = cs[:, :RWKV_WIDTH] + cs[:, RWKV_WIDTH:2 * RWKV_WIDTH] + cs[:, 2 * RWKV_WIDTH:]
    tot = cum[C - 1:C, :]
    e_inv = jnp.exp(-cum)
    e_tail = jnp.exp(tot - cum)
    at_all = a_ref[0] * jnp.exp(cum - lw)
    rt_all = r_ref[0] * jnp.exp(cum)
    bt_all = b_ref[0] * e_inv
    kt_all = k_ref[0] * e_inv
    bh_all = b_ref[0] * e_tail
    kh_all = k_ref[0] * e_tail
    w_tot = jnp.exp(tot)
    v_all = v_ref[0]

    row = lax.broadcasted_iota(jnp.int32, (2 * C, 2 * C), 0)
    col = lax.broadcasted_iota(jnp.int32, (2 * C, 2 * C), 1)
    same = (row // C) == (col // C)
    strict = same & (col < row)
    incl = same & (col <= row)
    eye = jnp.where(row == col, 1.0, 0.0)
    lane = lax.broadcasted_iota(jnp.int32, (C, 128), 1)
    first = lane < RWKV_HEAD
    bf = lambda x: x.astype(BF16)
    ls = [slice(pi * 128, (pi + 1) * 128) for pi in pairs]

    def stack(x):
        return jnp.concatenate([jnp.where(first, x, 0.0), jnp.where(first, 0.0, x)], axis=0)

    def dup(x):
        return jnp.concatenate([x, x], axis=0)

    at_s = [stack(at_all[:, s]) for s in ls]
    rt_s = [stack(rt_all[:, s]) for s in ls]
    v_s = [stack(v_all[:, s]) for s in ls]
    g = [_dot_nt(bf(jnp.concatenate([at_s[i], rt_s[i]], axis=0)),
                 bf(jnp.concatenate([dup(bt_all[:, ls[i]]), dup(kt_all[:, ls[i]])], axis=0))) for i in pairs]
    l_ab = [jnp.where(strict, g[i][0:2 * C, 0:2 * C], 0.0) for i in pairs]
    a_ak = [jnp.where(strict, g[i][0:2 * C, 2 * C:4 * C], 0.0) for i in pairs]
    a_rb = [jnp.where(incl, g[i][2 * C:4 * C, 0:2 * C], 0.0) for i in pairs]
    a_rk = [jnp.where(incl, g[i][2 * C:4 * C, 2 * C:4 * C], 0.0) for i in pairs]
    x = [eye + l for l in l_ab]
    pw = l_ab
    for _ in range(int(math.log2(C)) - 1):
        pw = [_dot(bf(m), bf(m)) for m in pw]
        x = [x[i] + _dot(bf(x[i]), bf(pw[i])) for i in pairs]
    av = [_dot(bf(jnp.concatenate([a_ak[i], a_rk[i]], axis=0)), bf(v_s[i])) for i in pairs]
    tx = [_dot(bf(x[i]), bf(jnp.concatenate([at_s[i], av[i][0:2 * C]], axis=1))) for i in pairs]
    s_old = [s_scr[i] for i in pairs]
    az = [_dot_nt(bf(jnp.concatenate([tx[i][:, 0:128], rt_s[i]], axis=0)), bf(s_old[i])) for i in pairs]
    u = [az[i][0:2 * C] + tx[i][:, 128:256] for i in pairs]
    y = [az[i][2 * C:4 * C] + _dot(bf(a_rb[i]), bf(u[i])) + av[i][2 * C:4 * C] for i in pairs]
    for i in pairs:
        y_ref[0, :, ls[i]] = y[i][0:C] + y[i][C:2 * C]
    upd = [_dot_tn(bf(jnp.concatenate([u[i], v_s[i]], axis=0)),
                   bf(jnp.concatenate([stack(bh_all[:, ls[i]]), stack(kh_all[:, ls[i]])], axis=0))) for i in pairs]
    for i in pairs:
        s_scr[i] = s_old[i] * w_tot[:, ls[i]] + upd[i]

    @pl.when(c == nc - 1)
    def _():
        sout_ref[0] = s_scr[...]


def _wkv(r, lw, k, v, a, b, s0_bd):
    n_seq, t, _ = r.shape
    seq_spec = pl.BlockSpec((1, CHUNK, RWKV_WIDTH), lambda s, c: (s, c, 0))
    st_spec = pl.BlockSpec((1, RWKV_PAIRS, 128, 128), lambda s, c: (s, 0, 0, 0))
    return pl.pallas_call(
        _wkv_kernel,
        grid=(n_seq, t // CHUNK),
        in_specs=[seq_spec] * 6 + [st_spec],
        out_specs=[seq_spec, st_spec],
        out_shape=[jax.ShapeDtypeStruct((n_seq, t, RWKV_WIDTH), F32),
                   jax.ShapeDtypeStruct((n_seq, RWKV_PAIRS, 128, 128), F32)],
        scratch_shapes=[pltpu.VMEM((RWKV_PAIRS, 128, 128), F32)],
        compiler_params=_params("arbitrary", "arbitrary"),
        name="wkv_scan",
    )(r, lw, k, v, a, b, s0_bd)


def _outproj_kernel(att_ref, y_ref, g_ref, bonus_ref, lng_ref, lnb_ref, ones_ref, wo_ref, x_ref, mod_ref,
                    gpost_ref, o_ref):
    y = y_ref[...]
    ones_bd = ones_ref[...]
    inv_n = 1.0 / RWKV_HEAD
    yc = y - _dot(y.astype(BF16), ones_bd) * inv_n
    var = _dot((yc * yc).astype(BF16), ones_bd) * inv_n
    yn = yc * lax.rsqrt(var + LNX_EPS)
    rw = (yn * lng_ref[...] + lnb_ref[...] + bonus_ref[...]) * g_ref[...]
    mix = (_dot(att_ref[...].astype(BF16), wo_ref[0:ATT_WIDTH, :])
           + _dot(rw.astype(BF16), wo_ref[ATT_WIDTH:, :]))
    o_ref[...] = x_ref[...] + mod_ref[0, 2] * _rms(mix, gpost_ref[...], RMS_EPS)


def _outproj(att, y, g, bonus, lng, lnb, ones_bd, w_out_b, x, mod4, g_post, tm, tiles_per_mod):
    m = x.shape[0]
    r = mod4.shape[2]
    tile = lambda w: pl.BlockSpec((tm, w), lambda i: (i, 0))
    const = lambda a: pl.BlockSpec(a.shape, lambda i: (0,) * a.ndim)
    return pl.pallas_call(
        _outproj_kernel,
        grid=(m // tm,),
        in_specs=[tile(ATT_WIDTH)] * 4 + [const(lng), const(lnb), const(ones_bd), const(w_out_b),
                                          tile(D_MODEL),
                                          pl.BlockSpec((1, 6, r, D_MODEL), lambda i: (i // tiles_per_mod, 0, 0, 0)),
                                          const(g_post)],
        out_specs=tile(D_MODEL),
        out_shape=jax.ShapeDtypeStruct((m, D_MODEL), F32),
        compiler_params=_params("arbitrary"),
        name="out_proj",
    )(att, y, g, bonus, lng, lnb, ones_bd, w_out_b, x, mod4, g_post)


FFN_TF = 1024


def _ffn_kernel(x_ref, mod_ref, gpre_ref, gpost_ref, wu_ref, wd_ref, o_ref, h_scr, acc_scr):
    f = pl.program_id(1)

    @pl.when(f == 0)
    def _():
        h = _rms(x_ref[...], gpre_ref[...], RMS_EPS) * (1.0 + mod_ref[0, 4]) + mod_ref[0, 3]
        h_scr[...] = h.astype(BF16)
        acc_scr[...] = jnp.zeros(acc_scr.shape, F32)

    u = jnp.maximum(_dot(h_scr[...], wu_ref[...]), 0.0)
    acc_scr[...] += _dot((u * u).astype(BF16), wd_ref[...])

    @pl.when(f == pl.num_programs(1) - 1)
    def _():
        o_ref[...] = x_ref[...] + mod_ref[0, 5] * _rms(acc_scr[...], gpost_ref[...], RMS_EPS)


def _ffn(x, mod4, g_pre, g_post, w_up_b, w_down_b, tm, tiles_per_mod):
    m = x.shape[0]
    r = mod4.shape[2]
    return pl.pallas_call(
        _ffn_kernel,
        grid=(m // tm, FFN_DIM // FFN_TF),
        in_specs=[pl.BlockSpec((tm, D_MODEL), lambda i, f: (i, 0)),
                  pl.BlockSpec((1, 6, r, D_MODEL), lambda i, f: (i // tiles_per_mod, 0, 0, 0)),
                  pl.BlockSpec((1, D_MODEL), lambda i, f: (0, 0)),
                  pl.BlockSpec((1, D_MODEL), lambda i, f: (0, 0)),
                  pl.BlockSpec((D_MODEL, FFN_TF), lambda i, f: (0, f)),
                  pl.BlockSpec((FFN_TF, D_MODEL), lambda i, f: (f, 0))],
        out_specs=pl.BlockSpec((tm, D_MODEL), lambda i, f: (i, 0)),
        out_shape=jax.ShapeDtypeStruct((m, D_MODEL), F32),
        scratch_shapes=[pltpu.VMEM((tm, D_MODEL), BF16), pltpu.VMEM((tm, D_MODEL), F32)],
        compiler_params=_params("arbitrary", "arbitrary"),
        name="ffn",
    )(x, mod4, g_pre, g_post, w_up_b, w_down_b)


FFN_TF_FUSED = 256


def _ffn_sattn_kernel(cfg, pt_ref, x_ref, mod_ref, gpre_ref, gpost_ref, wu_ref, wd_ref,
                      tbl_ref, lamv_ref, bkt_ref, subg_ref, kmask_ref, q_ref, kn_ref, vn_ref, *rest):
    g_pages = PAGES_PER_STEP
    k_refs = rest[:g_pages]
    v_refs = rest[g_pages:2 * g_pages]
    y_ref, o_ref, h_scr, bias_scr, m_scr, l_scr, acc_scr = rest[2 * g_pages:]
    nf, nj, n_ffn, n_att = cfg
    del pt_ref
    n_steps = max(n_ffn, n_att)
    s = pl.program_id(0)
    f = s % nf
    j = s % nj
    ffn_live = None if n_ffn == n_steps else s < n_ffn
    att_live = None if n_att == n_steps else s < n_att

    def also(cond, live):
        return cond if live is None else cond & live

    def run(live, fn):
        if live is None:
            fn()
        else:
            pl.when(live)(fn)

    @pl.when(also(f == 0, ffn_live))
    def _():
        h = _rms(x_ref[...], gpre_ref[...], RMS_EPS) * (1.0 + mod_ref[0, 4]) + mod_ref[0, 3]
        h_scr[...] = h.astype(BF16)
        y_ref[...] = jnp.zeros(y_ref.shape, F32)

    @pl.when(s == 0)
    def _():
        _sattn_build_bias(bias_scr, bkt_ref, tbl_ref)

    @pl.when(also(j == 0, att_live))
    def _():
        m_scr[...] = jnp.full(m_scr.shape, NEG_INF, F32)
        l_scr[...] = jnp.zeros(l_scr.shape, F32)
        acc_scr[...] = jnp.zeros(acc_scr.shape, F32)

    def ffn_main():
        u = jnp.maximum(_dot(h_scr[...], wu_ref[...]), 0.0)
        y_ref[...] += _dot((u * u).astype(BF16), wd_ref[...])

    def att_scores():
        return _sattn_scores(q_ref[0], kmask_ref[...], k_refs, bias_scr[(j == nj - 1).astype(jnp.int32)])

    def att_softmax(z):
        vall = jnp.concatenate([vr[0].astype(BF16) for vr in v_refs], axis=0)
        m_new, l_new, acc_new = _sattn_update(z, vall, m_scr[...], l_scr[...], acc_scr[...])
        m_scr[...] = m_new
        l_scr[...] = l_new
        acc_scr[...] = acc_new

    if ffn_live is None and att_live is None:
        z = att_scores()
        ffn_main()
        att_softmax(z)
    else:
        run(ffn_live, ffn_main)
        run(att_live, lambda: att_softmax(att_scores()))

    @pl.when(also(f == nf - 1, ffn_live))
    def _():
        y_ref[...] = x_ref[...] + mod_ref[0, 5] * _rms(y_ref[...], gpost_ref[...], RMS_EPS)

    @pl.when(also(j == nj - 1, att_live))
    def _():
        _sattn_finish(q_ref[0], kmask_ref[...], kn_ref, vn_ref, tbl_ref, lamv_ref, subg_ref, o_ref,
                      (m_scr[...], l_scr[...], acc_scr[...]))


def _ffn_sattn(x, mod4, g_pre, g_post, w_up_b, w_down_b, tm, tiles_per_mod,
               qaug_t, k_new, v_new, cache_k, cache_v, page_table, tbl, lamv, subg):
    m = x.shape[0]
    r = mod4.shape[2]
    db, new_rows = k_new.shape[0], k_new.shape[1]
    t_new = new_rows // ATT_HEADS
    g_pages = PAGES_PER_STEP
    nf = FFN_DIM // FFN_TF_FUSED
    nj = page_table.shape[1] // g_pages
    n_ffn, n_att = (m // tm) * nf, db * nj
    n_steps = max(n_ffn, n_att)
    tok = (np.arange(PAGE_ROWS) // ATT_HEADS)[:, None]
    bkt = jnp.asarray(_t5_bucket_np(PAGE + (np.arange(QCOLS) % 8)[None, :] - tok))
    key_head = (np.arange(PAGE_ROWS) % ATT_HEADS)[:, None]
    feat = np.arange(HEAD_V)[None, :]
    kmask = jnp.asarray(np.where((feat < ATT_HEADS) & (feat != key_head), NEG_INF, 0.0).astype(np.float32)).astype(BF16)

    sf = lambda s: jnp.minimum(s, n_ffn - 1)
    sa = lambda s: jnp.minimum(s, n_att - 1)

    def page_spec(g):
        return pl.BlockSpec((1, PAGE_ROWS, HEAD_V), lambda s, pt: (pt[sa(s) // nj, (sa(s) % nj) * g_pages + g], 0, 0))

    const = lambda shape: pl.BlockSpec(shape, lambda s, pt: (0,) * len(shape))
    samp = lambda shape: pl.BlockSpec(shape, lambda s, pt: (sa(s) // nj,) + (0,) * (len(shape) - 1))
    grid_spec = pltpu.PrefetchScalarGridSpec(
        num_scalar_prefetch=1,
        grid=(n_steps,),
        in_specs=[pl.BlockSpec((tm, D_MODEL), lambda s, pt: (sf(s) // nf, 0)),
                  pl.BlockSpec((1, 6, r, D_MODEL), lambda s, pt: (sf(s) // nf // tiles_per_mod, 0, 0, 0)),
                  const((1, D_MODEL)), const((1, D_MODEL)),
                  pl.BlockSpec((D_MODEL, FFN_TF_FUSED), lambda s, pt: (0, sf(s) % nf)),
                  pl.BlockSpec((FFN_TF_FUSED, D_MODEL), lambda s, pt: (sf(s) % nf, 0)),
                  pl.BlockSpec(memory_space=pltpu.SMEM),
                  const((4, HEAD_QK)), const((PAGE_ROWS, QCOLS)), const((1, HEAD_V)), const((PAGE_ROWS, HEAD_V)),
                  samp((1, 2 * HEAD_V, QCOLS)), samp((1, new_rows, HEAD_V)), samp((1, new_rows, HEAD_V))]
                 + [page_spec(g) for g in range(g_pages)]
                 + [page_spec(g) for g in range(g_pages)],
        out_specs=[pl.BlockSpec((tm, D_MODEL), lambda s, pt: (sf(s) // nf, 0)),
                   samp((1, t_new, ATT_WIDTH))],
        scratch_shapes=[pltpu.VMEM((tm, D_MODEL), BF16),
                        pltpu.VMEM((2, PAGE_ROWS, QCOLS), F32),
                        pltpu.VMEM((8, QCOLS), F32),
                        pltpu.VMEM((8, QCOLS), F32),
                        pltpu.VMEM((HEAD_V, QCOLS), F32)],
    )
    return pl.pallas_call(
        functools.partial(_ffn_sattn_kernel, (nf, nj, n_ffn, n_att)),
        grid_spec=grid_spec,
        out_shape=[jax.ShapeDtypeStruct((m, D_MODEL), F32),
                   jax.ShapeDtypeStruct((db, t_new, ATT_WIDTH), F32)],
        compiler_params=_params("arbitrary"),
        name="ffn_sattn",
    )(page_table, x, mod4, g_pre, g_post, w_up_b, w_down_b, tbl, lamv, bkt, subg, kmask, qaug_t, k_new, v_new,
      *([cache_k] * g_pages), *([cache_v] * g_pages))


def _state_to_blockdiag(s):
    n = s.shape[0]
    s = s.reshape(n, RWKV_PAIRS, 2, RWKV_HEAD, RWKV_HEAD)
    z = jnp.zeros((n, RWKV_PAIRS, RWKV_HEAD, RWKV_HEAD), s.dtype)
    top = jnp.concatenate([s[:, :, 0], z], axis=-1)
    bot = jnp.concatenate([z, s[:, :, 1]], axis=-1)
    return jnp.concatenate([top, bot], axis=-2)


def _blockdiag_to_state(sbd):
    n = sbd.shape[0]
    s = jnp.stack([sbd[:, :, :RWKV_HEAD, :RWKV_HEAD], sbd[:, :, RWKV_HEAD:, RWKV_HEAD:]], axis=2)
    return s.reshape(n, 2 * RWKV_PAIRS, RWKV_HEAD, RWKV_HEAD)


def _pad_cols(a, width):
    return jnp.pad(a, ((0, 0), (0, width - a.shape[1])))


def kernel(x_prompt, x_sample, cache_k, cache_v, state_wkv, state_shift, page_table, c_prompt, c_sample, bias_table, w_ada, b_ada, g_pre_mix, g_post_mix, g_pre_ffn, g_post_ffn, w_in, mu_shift, w0, w_lora_w, a0, w_lora_a, w_lora_g, k_k, k_a, r_k, lnx_g, lnx_b, lam_q1, lam_k1, lam_q2, lam_k2, subln_g, w_out, w_ffn_up, w_ffn_down):
    bsz, seq, d = x_prompt.shape
    dbs, tdec, _ = x_sample.shape
    n_p, n_s = bsz * seq, dbs * tdec

    w_in_b = jnp.concatenate([w_in[0], jnp.zeros((d, P_PAD - RWKV_PROJ), F32)], axis=1).astype(BF16)
    w_out_b = w_out[0].astype(BF16)
    w_up_b = w_ffn_up[0].astype(BF16)
    w_down_b = w_ffn_down[0].astype(BF16)
    wwa = jnp.zeros((128, 2 * RWKV_WIDTH), F32)
    wwa = wwa.at[:64, :RWKV_WIDTH].set(w_lora_w[0]).at[64:, RWKV_WIDTH:].set(w_lora_a[0]).astype(BF16)
    wg = jnp.pad(w_lora_g[0], ((0, P_PAD - GATE_IN - w_lora_g.shape[1]), (0, 0))).astype(BF16)
    head_id = jnp.arange(RWKV_WIDTH) // RWKV_HEAD
    ones_bd = (head_id[:, None] == head_id[None, :]).astype(BF16)
    prep_consts = (_pad_cols(mu_shift, P_PAD), w0, a0, k_k, k_a, r_k.reshape(1, RWKV_WIDTH), wwa, wg, ones_bd)
    lamv = jnp.concatenate([lam_q1, lam_k1, lam_q2, lam_k2], axis=0)

    n_c = bsz + dbs
    c_all = jnp.pad(jnp.concatenate([c_prompt, c_sample], axis=0), ((0, (-n_c) % 8), (0, 0)))
    mod = _ada(c_all, w_ada[0], b_ada)
    mod_p = mod[:bsz].reshape(bsz, 6, 1, d)
    mod_s = jnp.repeat(mod[bsz:n_c].reshape(dbs, 6, d), tdec, axis=0).transpose(1, 0, 2).reshape(1, 6, n_s, d)

    xp = x_prompt.reshape(n_p, d)
    xs = x_sample.reshape(n_s, d)
    tm_p = min(1024, seq)
    q_p, k_p, v_p, kb_p, vb_p, p_p = _inproj(xp, mod_p, g_pre_mix, w_in_b, tm_p, seq // tm_p)
    q_s, k_s, v_s, _, _, p_s = _inproj(xs, mod_s, g_pre_mix, w_in_b, n_s, 1)

    att_p = _prompt_attn(q_p.reshape(bsz, seq, ATT_WIDTH), kb_p.reshape(bsz, seq, ATT_WIDTH),
                         vb_p.reshape(bsz, seq, ATT_WIDTH), bias_table, lamv, subln_g)
    q4 = q_s.reshape(dbs, tdec, ATT_HEADS, HEAD_V).transpose(0, 2, 1, 3)
    first = jnp.arange(HEAD_V) < HEAD_QK
    zq = jnp.zeros((dbs, ATT_HEADS, 8 - tdec, HEAD_V), BF16)
    q_rows = jnp.concatenate([jnp.where(first, q4, 0), zq, jnp.where(first, 0, q4), zq], axis=2)
    head_feat = (jnp.arange(QCOLS)[:, None] // QROWS == jnp.arange(HEAD_V)[None, :]).astype(BF16)
    qaug_t = jnp.concatenate([q_rows.reshape(dbs, QCOLS, HEAD_V),
                              jnp.broadcast_to(head_feat, (dbs, QCOLS, HEAD_V))], axis=2).swapaxes(1, 2)
    n_pool = cache_k.shape[1]

    pre_p = _rwkv_prep(p_p, None, prep_consts, bsz, min(256, seq))
    p_s3 = p_s.reshape(dbs, tdec, P_PAD)
    shift_s = jnp.concatenate([_pad_cols(state_shift[0], P_PAD)[:, None, :], p_s3[:, :-1]], axis=1)
    pre_s = _rwkv_prep(p_s, shift_s.reshape(n_s, P_PAD), prep_consts, dbs, n_s)
    r_p, lw_p, kk_p, vv_p, a_p, b_p, g_p, bonus_p = pre_p
    r_s, lw_s, kk_s, vv_s, a_s, b_s, g_s, bonus_s = pre_s
    seq3 = lambda t: t.reshape(bsz, seq, RWKV_WIDTH)
    y_p, sbd_p = _wkv(seq3(r_p), seq3(lw_p), seq3(kk_p), seq3(vv_p), seq3(a_p), seq3(b_p),
                      jnp.zeros((bsz, RWKV_PAIRS, 128, 128), F32))
    dec3 = lambda t: jnp.pad(t.reshape(dbs, tdec, RWKV_WIDTH), ((0, 0), (0, CHUNK - tdec), (0, 0)))
    y_s, sbd_s = _wkv(dec3(r_s), dec3(lw_s), dec3(kk_s), dec3(vv_s), dec3(a_s), dec3(b_s),
                      _state_to_blockdiag(state_wkv[0]))
    y_s = y_s[:, :tdec].reshape(n_s, RWKV_WIDTH)

    tm_o = min(256, seq)
    x1_p = _outproj(att_p.reshape(n_p, ATT_WIDTH), y_p.reshape(n_p, RWKV_WIDTH), g_p, bonus_p, lnx_g, lnx_b,
                    ones_bd, w_out_b, xp, mod_p, g_post_mix, tm_o, seq // tm_o)
    tm_f = min(512, seq)
    out_p, att_s = _ffn_sattn(x1_p, mod_p, g_pre_ffn, g_post_ffn, w_up_b, w_down_b, tm_f, seq // tm_f,
                              qaug_t, k_s.reshape(dbs, tdec * ATT_HEADS, HEAD_V), v_s.reshape(dbs, tdec * ATT_HEADS, HEAD_V),
                              cache_k[0].reshape(n_pool, PAGE_ROWS, HEAD_V), cache_v[0].reshape(n_pool, PAGE_ROWS, HEAD_V),
                              page_table, bias_table, lamv, subln_g)
    x1_s = _outproj(att_s.reshape(n_s, ATT_WIDTH), y_s, g_s, bonus_s, lnx_g, lnx_b,
                    ones_bd, w_out_b, xs, mod_s, g_post_mix, n_s, 1)
    out_s = _ffn(x1_s, mod_s, g_pre_ffn, g_post_ffn, w_up_b, w_down_b, n_s, 1)

    return (out_p.reshape(bsz, seq, d),
            out_s.reshape(dbs, tdec, d),
            k_p.reshape(1, bsz, seq, ATT_HEADS, HEAD_V),
            v_p.reshape(1, bsz, seq, ATT_HEADS, HEAD_V),
            _blockdiag_to_state(sbd_p)[None],
            p_p.reshape(bsz, seq, P_PAD)[None, :, -1, :RWKV_PROJ],
            k_s.reshape(1, dbs, tdec, ATT_HEADS, HEAD_V),
            v_s.reshape(1, dbs, tdec, ATT_HEADS, HEAD_V),
            _blockdiag_to_state(sbd_s)[None],
            p_s3[None, :, -1, :RWKV_PROJ])
```

```python
import functools
import math

import numpy as np
import jax
import jax.numpy as jnp
from jax import lax
from jax.experimental import pallas as pl
from jax.experimental.pallas import tpu as pltpu

F32 = jnp.float32
BF16 = jnp.bfloat16

D_MODEL = 2048
ATT_WIDTH = 1024
RWKV_WIDTH = 1024
ATT_HEADS = 8
HEAD_V = 128
HEAD_QK = 64
RWKV_HEAD = 64
RWKV_PAIRS = RWKV_WIDTH // 128
RWKV_PROJ = 3360
P_PAD = 3584
LORA_IN = 3072
GATE_IN = 3200
FFN_DIM = 8192
N_BUCKETS = 32
MAX_DISTANCE = 128
PAGE = 128
ATT_SCALE = HEAD_QK ** -0.5
RMS_EPS = 1e-6
SUBLN_EPS = 1e-5
LNX_EPS = 64e-5
NEG_INF = -1e30
LAMBDA_INIT = 0.8 - 0.6 * math.exp(-0.3 * 0)
CHUNK = 64
VMEM_LIMIT = 56 * 1024 * 1024


def _params(*sem):
    return pltpu.CompilerParams(dimension_semantics=sem, vmem_limit_bytes=VMEM_LIMIT)


def _dot(a, b):
    return jnp.dot(a, b, preferred_element_type=F32)


def _dot_nt(a, b):
    return lax.dot_general(a, b, (((1,), (1,)), ((), ())), preferred_element_type=F32)


def _dot_tn(a, b):
    return lax.dot_general(a, b, (((0,), (0,)), ((), ())), preferred_element_type=F32)


def _t5_bucket_np(dist):
    max_exact = N_BUCKETS // 2
    d = np.maximum(dist, 0)
    ratio = np.log(np.maximum(d, 1).astype(np.float32) / max_exact) / math.log(MAX_DISTANCE / max_exact)
    large = np.minimum(max_exact + (ratio * (N_BUCKETS - max_exact)).astype(np.int32), N_BUCKETS - 1)
    return np.where(d < max_exact, d, large).astype(np.int32)


def _table_lookup(bucket, tbl_ref, h):
    out = jnp.zeros(bucket.shape, F32)
    for b in range(N_BUCKETS):
        out = jnp.where(bucket == b, tbl_ref[b, h], out)
    return out


def _lam(lamv_ref):
    v = lamv_ref[...]
    s1 = jnp.sum(v[0:1] * v[1:2], axis=-1, keepdims=True)
    s2 = jnp.sum(v[2:3] * v[3:4], axis=-1, keepdims=True)
    return jnp.exp(s1) - jnp.exp(s2) + LAMBDA_INIT


def _rms(x, g, eps):
    return x * lax.rsqrt(jnp.mean(x * x, axis=-1, keepdims=True) + eps) * g


def _lanes(x, reps):
    return jnp.concatenate([x] * reps, axis=1)


def _ada_kernel(c_ref, w_ref, b_ref, o_ref):
    c = c_ref[...]
    x = (c * jax.nn.sigmoid(c)).astype(BF16)
    o_ref[...] = _dot(x, w_ref[...].astype(BF16)) + b_ref[...]


def _ada(c_all, w_ada, b_ada):
    rows, n = c_all.shape[0], w_ada.shape[1]
    tn = 1536
    return pl.pallas_call(
        _ada_kernel,
        grid=(n // tn,),
        in_specs=[pl.BlockSpec((rows, D_MODEL), lambda j: (0, 0)),
                  pl.BlockSpec((D_MODEL, tn), lambda j: (0, j)),
                  pl.BlockSpec((1, tn), lambda j: (0, j))],
        out_specs=pl.BlockSpec((rows, tn), lambda j: (0, j)),
        out_shape=jax.ShapeDtypeStruct((rows, n), F32),
        compiler_params=_params("arbitrary"),
        name="ada_mod",
    )(c_all, w_ada, b_ada)


IN_TN = 512


PROJ_W = P_PAD + 3 * ATT_WIDTH
NPJ = P_PAD // IN_TN


def _inproj_kernel(x_ref, mod_ref, g_ref, w_ref, proj_ref, qkvb_ref, h_scr):
    j = pl.program_id(1)

    @pl.when(j == 0)
    def _():
        h = _rms(x_ref[...], g_ref[...], RMS_EPS) * (1.0 + mod_ref[0, 1]) + mod_ref[0, 0]
        h_scr[...] = h.astype(BF16)

    proj_ref[...] = _dot(h_scr[...], w_ref[...])

    @pl.when(j >= NPJ)
    def _():
        scale = jnp.where(j < NPJ + ATT_WIDTH // IN_TN, ATT_SCALE, 1.0)
        qkvb_ref[...] = (proj_ref[...] * scale).astype(BF16)


def _inproj(x, mod4, g_pre, w_in_b, tm, tiles_per_mod):
    m = x.shape[0]
    r = mod4.shape[2]
    nj = PROJ_W // IN_TN
    return pl.pallas_call(
        _inproj_kernel,
        grid=(m // tm, nj),
        in_specs=[pl.BlockSpec((tm, D_MODEL), lambda i, j: (i, 0)),
                  pl.BlockSpec((1, 6, r, D_MODEL), lambda i, j: (i // tiles_per_mod, 0, 0, 0)),
                  pl.BlockSpec((1, D_MODEL), lambda i, j: (0, 0)),
                  pl.BlockSpec((D_MODEL, IN_TN), lambda i, j: (0, j))],
        out_specs=[pl.BlockSpec((tm, IN_TN), lambda i, j: (i, j)),
                   pl.BlockSpec((tm, IN_TN), lambda i, j: (i, jnp.maximum(j - NPJ, 0)))],
        out_shape=[jax.ShapeDtypeStruct((m, PROJ_W), F32),
                   jax.ShapeDtypeStruct((m, 3 * ATT_WIDTH), BF16)],
        scratch_shapes=[pltpu.VMEM((tm, D_MODEL), BF16)],
        compiler_params=_params("arbitrary", "arbitrary"),
        name="in_proj",
    )(x, mod4, g_pre, w_in_b)


ATT_T = 512
ATT_SUB = 128


def _pattn_kernel(qi_ref, ki_ref, tbl_ref, lamv_ref, bkt_ref, subg_ref, q_ref, k_ref, v_ref, o_ref,
                  bias_scr, q2_scr, m_scr, l_scr, acc_scr):
    h = pl.program_id(1)
    step = pl.program_id(2)
    qi = qi_ref[step]
    ki = ki_ref[step]
    nsub = ATT_T // ATT_SUB

    @pl.when(step == 0)
    def _():
        row = lax.broadcasted_iota(jnp.int32, (ATT_SUB, ATT_SUB), 0)
        col = lax.broadcasted_iota(jnp.int32, (ATT_SUB, ATT_SUB), 1)
        t0 = jnp.where(col > row, NEG_INF, _table_lookup(bkt_ref[0], tbl_ref, h))
        t1 = _table_lookup(bkt_ref[1], tbl_ref, h)
        far = jnp.full((ATT_SUB, ATT_SUB), tbl_ref[N_BUCKETS - 1, h], F32)
        masked = jnp.full((ATT_SUB, ATT_SUB), NEG_INF, F32)
        for rb in range(nsub):
            for cb in range(nsub):
                d = rb - cb
                diag = t0 if d == 0 else t1 if d == 1 else far if d >= 2 else masked
                off = t1 if (rb == 0 and cb == nsub - 1) else far
                rs, cs = slice(rb * ATT_SUB, (rb + 1) * ATT_SUB), slice(cb * ATT_SUB, (cb + 1) * ATT_SUB)
                bias_scr[0, rs, cs] = diag
                bias_scr[1, rs, cs] = off
                bias_scr[2, rs, cs] = far

    @pl.when(ki == 0)
    def _():
        q = q_ref[0]
        lane = lax.broadcasted_iota(jnp.int32, q.shape, 1)
        zero = jnp.zeros_like(q)
        q2_scr[0:ATT_T, :] = jnp.where(lane < HEAD_QK, q, zero)
        q2_scr[ATT_T:, :] = jnp.where(lane >= HEAD_QK, q, zero)
        m_scr[...] = jnp.full(m_scr.shape, NEG_INF, F32)
        l_scr[...] = jnp.zeros(l_scr.shape, F32)
        acc_scr[...] = jnp.zeros(acc_scr.shape, F32)

    bias = bias_scr[jnp.minimum(qi - ki, 2)]
    s = _dot_nt(q2_scr[...], k_ref[0]) + jnp.concatenate([bias, bias], axis=0)
    m_prev = m_scr[...]
    m_new = jnp.maximum(m_prev, jnp.max(s, axis=-1, keepdims=True))
    alpha = jnp.exp(m_prev - m_new)
    p = jnp.exp(s - _lanes(m_new, ATT_T // 128))
    l_scr[...] = alpha * l_scr[...] + jnp.sum(p, axis=-1, keepdims=True)
    acc_scr[...] = alpha * acc_scr[...] + _dot(p.astype(BF16), v_ref[0])
    m_scr[...] = m_new

    @pl.when(ki == qi)
    def _():
        on = acc_scr[...] / l_scr[...]
        o = on[0:ATT_T] - _lam(lamv_ref) * on[ATT_T:]
        o = _rms(o, subg_ref[...], SUBLN_EPS) * (1.0 - LAMBDA_INIT)
        o_ref[0] = o.astype(BF16)


def _prompt_attn(qkv, tbl, lamv, subg):
    b, s, _ = qkv.shape
    nq = s // ATT_T
    r = np.arange(ATT_SUB)
    dist = r[:, None] - r[None, :]
    bkt = jnp.asarray(np.stack([_t5_bucket_np(dist), _t5_bucket_np(dist + ATT_SUB)]))
    pairs = [(qi, ki) for qi in range(nq) for ki in range(qi + 1)]
    qi_of = jnp.asarray(np.array([p[0] for p in pairs], np.int32))
    ki_of = jnp.asarray(np.array([p[1] for p in pairs], np.int32))
    q_spec = pl.BlockSpec((1, ATT_T, HEAD_V), lambda bi, h, st, qi, ki: (bi, qi[st], h))
    k_spec = pl.BlockSpec((1, ATT_T, HEAD_V), lambda bi, h, st, qi, ki: (bi, ki[st], ATT_HEADS + h))
    v_spec = pl.BlockSpec((1, ATT_T, HEAD_V), lambda bi, h, st, qi, ki: (bi, ki[st], 2 * ATT_HEADS + h))
    const = lambda shape: pl.BlockSpec(shape, lambda bi, h, st, qi, ki: (0,) * len(shape))
    grid_spec = pltpu.PrefetchScalarGridSpec(
        num_scalar_prefetch=2,
        grid=(b, ATT_HEADS, len(pairs)),
        in_specs=[pl.BlockSpec(memory_space=pltpu.SMEM), const((4, HEAD_QK)), const((2, ATT_SUB, ATT_SUB)),
                  const((1, HEAD_V)), q_spec, k_spec, v_spec],
        out_specs=q_spec,
        scratch_shapes=[pltpu.VMEM((3, ATT_T, ATT_T), F32),
                        pltpu.VMEM((2 * ATT_T, HEAD_V), BF16),
                        pltpu.VMEM((2 * ATT_T, 128), F32),
                        pltpu.VMEM((2 * ATT_T, 128), F32),
                        pltpu.VMEM((2 * ATT_T, HEAD_V), F32)],
    )
    return pl.pallas_call(
        _pattn_kernel,
        grid_spec=grid_spec,
        out_shape=jax.ShapeDtypeStruct((b, s, ATT_WIDTH), BF16),
        compiler_params=_params("arbitrary", "arbitrary", "arbitrary"),
        name="prompt_attn",
    )(qi_of, ki_of, tbl, lamv, bkt, subg, qkv, qkv, qkv)


PAGES_PER_STEP = 16
QROWS = 16
PAGE_ROWS = PAGE * ATT_HEADS


def _sattn_kernel(pt_ref, tbl_ref, lamv_ref, bkt_ref, subg_ref, q_ref, kn_ref, vn_ref, *rest):
    g_pages = PAGES_PER_STEP
    k_refs = rest[:g_pages]
    v_refs = rest[g_pages:2 * g_pages]
    o_ref, bias_scr, m_scr, l_scr, acc_scr = rest[2 * g_pages:]
    del pt_ref
    b = pl.program_id(0)
    j = pl.program_id(1)
    nj = pl.num_programs(1)
    n_new = kn_ref.shape[1]
    nrow = ATT_HEADS * QROWS

    @pl.when((b == 0) & (j == 0))
    def _():
        key_head = lax.broadcasted_iota(jnp.int32, (QROWS, PAGE_ROWS), 1) % ATT_HEADS
        for h in range(ATT_HEADS):
            rows = slice(h * QROWS, (h + 1) * QROWS)
            bias_scr[0, rows, :] = jnp.where(key_head == h, tbl_ref[N_BUCKETS - 1, h], NEG_INF)
            bias_scr[1, rows, :] = jnp.where(key_head == h, _table_lookup(bkt_ref[...], tbl_ref, h), NEG_INF)

    @pl.when(j == 0)
    def _():
        m_scr[...] = jnp.full(m_scr.shape, NEG_INF, F32)
        l_scr[...] = jnp.zeros(l_scr.shape, F32)
        acc_scr[...] = jnp.zeros(acc_scr.shape, F32)

    q = q_ref[0]
    s_pages = []
    for g in range(g_pages):
        bias = bias_scr[(j == nj - 1).astype(jnp.int32)] if g == g_pages - 1 else bias_scr[0]
        s_pages.append(_dot_nt(q, k_refs[g][0].astype(BF16)) + bias)
    m_run = m_scr[...]
    m_new = m_run
    for s in s_pages:
        m_new = jnp.maximum(m_new, jnp.max(s, axis=-1, keepdims=True))
    alpha = jnp.exp(m_run - m_new)
    m_lanes = _lanes(m_new, PAGE_ROWS // 128)
    l_run = alpha * l_scr[...]
    acc = alpha * acc_scr[...]
    for g in range(g_pages):
        p = jnp.exp(s_pages[g] - m_lanes)
        l_run = l_run + jnp.sum(p, axis=-1, keepdims=True)
        acc = acc + _dot(p.astype(BF16), v_refs[g][0].astype(BF16))
    m_run = m_new
    m_scr[...] = m_run
    l_scr[...] = l_run
    acc_scr[...] = acc

    @pl.when(j == nj - 1)
    def _():
        lam = _lam(lamv_ref)
        qf = q.astype(F32)
        t_row = lax.broadcasted_iota(jnp.int32, (nrow, 1), 0) % 8
        head_of_row = lax.broadcasted_iota(jnp.int32, (nrow, 1), 0) // QROWS
        m_fin, l_fin, acc_fin = m_run, l_run, acc
        tbl_rows = []
        for dd in range(n_new):
            tbl_row = jnp.zeros((nrow, 1), F32)
            for h in range(ATT_HEADS):
                tbl_row = jnp.where(head_of_row == h, tbl_ref[dd, h], tbl_row)
            tbl_rows.append(tbl_row)
        for tn in range(n_new):
            d = t_row - tn
            bias = jnp.full((nrow, 1), NEG_INF, F32)
            for dd in range(n_new):
                bias = jnp.where(d == dd, tbl_rows[dd], bias)
            k_rows = jnp.concatenate(
                [jnp.broadcast_to(kn_ref[0, tn:tn + 1, h * HEAD_V:(h + 1) * HEAD_V], (QROWS, HEAD_V))
                 for h in range(ATT_HEADS)], axis=0)
            v_rows = jnp.concatenate(
                [jnp.broadcast_to(vn_ref[0, tn:tn + 1, h * HEAD_V:(h + 1) * HEAD_V], (QROWS, HEAD_V))
                 for h in range(ATT_HEADS)], axis=0)
            s = jnp.sum(qf * k_rows, axis=-1, keepdims=True) + bias
            m_new = jnp.maximum(m_fin, s)
            alpha = jnp.exp(m_fin - m_new)
            p = jnp.exp(s - m_new)
            l_fin = alpha * l_fin + p
            acc_fin = alpha * acc_fin + p * v_rows
            m_fin = m_new
        on = acc_fin / l_fin
        for h in range(ATT_HEADS):
            o = on[h * QROWS:h * QROWS + 8] - lam * on[h * QROWS + 8:(h + 1) * QROWS]
            o = _rms(o, subg_ref[...], SUBLN_EPS) * (1.0 - LAMBDA_INIT)
            o_ref[0, :, h * HEAD_V:(h + 1) * HEAD_V] = o[0:n_new]


def _sample_attn(q_rows, k_new, v_new, cache_k, cache_v, page_table, tbl, lamv, subg):
    db, t_new = k_new.shape[0], k_new.shape[1]
    n_pages = page_table.shape[1]
    g_pages = PAGES_PER_STEP
    nrow = ATT_HEADS * QROWS
    t_row = (np.arange(QROWS) % 8)[:, None]
    tok = (np.arange(PAGE_ROWS) // ATT_HEADS)[None, :]
    bkt = jnp.asarray(_t5_bucket_np(PAGE + t_row - tok))

    def page_spec(g):
        return pl.BlockSpec((1, PAGE_ROWS, HEAD_V), lambda b, j, pt: (pt[b, j * g_pages + g], 0, 0))

    new_spec = pl.BlockSpec((1, t_new, ATT_WIDTH), lambda b, j, pt: (b, 0, 0))
    grid_spec = pltpu.PrefetchScalarGridSpec(
        num_scalar_prefetch=1,
        grid=(db, n_pages // g_pages),
        in_specs=[pl.BlockSpec(memory_space=pltpu.SMEM),
                  pl.BlockSpec((4, HEAD_QK), lambda b, j, pt: (0, 0)),
                  pl.BlockSpec((QROWS, PAGE_ROWS), lambda b, j, pt: (0, 0)),
                  pl.BlockSpec((1, HEAD_V), lambda b, j, pt: (0, 0)),
                  pl.BlockSpec((1, nrow, HEAD_V), lambda b, j, pt: (b, 0, 0)),
                  new_spec, new_spec]
                 + [page_spec(g) for g in range(g_pages)]
                 + [page_spec(g) for g in range(g_pages)],
        out_specs=new_spec,
        scratch_shapes=[pltpu.VMEM((2, nrow, PAGE_ROWS), F32),
                        pltpu.VMEM((nrow, 128), F32),
                        pltpu.VMEM((nrow, 128), F32),
                        pltpu.VMEM((nrow, HEAD_V), F32)],
    )
    return pl.pallas_call(
        _sattn_kernel,
        grid_spec=grid_spec,
        out_shape=jax.ShapeDtypeStruct((db, t_new, ATT_WIDTH), F32),
        compiler_params=_params("arbitrary", "arbitrary"),
        name="sample_attn",
    )(page_table, tbl, lamv, bkt, subg, q_rows, k_new, v_new,
      *([cache_k] * g_pages), *([cache_v] * g_pages))


def _prep_math(p, ps, mu, w0, a0, kkw, kaw, rkw, wwa, wg, ones_bd, out_refs):
    pm = p + mu * (ps - p)
    r = pm[:, 0:RWKV_WIDTH]
    kr = pm[:, RWKV_WIDTH:2 * RWKV_WIDTH]
    v = pm[:, 2 * RWKV_WIDTH:3 * RWKV_WIDTH]
    wa = pm[:, LORA_IN:LORA_IN + 128]
    lane = lax.broadcasted_iota(jnp.int32, wa.shape, 1)
    la = _dot(jnp.where(lane < 64, jnp.tanh(wa), wa).astype(BF16), wwa)
    z = -(w0 + la[:, :RWKV_WIDTH])
    softplus = jnp.maximum(z, 0.0) + jnp.log(1.0 + jnp.exp(-jnp.abs(z)))
    log_decay = -jnp.exp(-softplus - 0.5)
    asig = jax.nn.sigmoid(a0 + la[:, RWKV_WIDTH:])
    g = _dot(jax.nn.sigmoid(pm[:, GATE_IN:P_PAD]).astype(BF16), wg)
    kk = kr * kkw
    norm = jnp.sqrt(_dot((kk * kk).astype(BF16), ones_bd))
    kk = kk / jnp.maximum(norm, 1e-12)
    k2 = kr * (1.0 + (asig - 1.0) * kaw)
    bonus = _dot((r * k2 * rkw).astype(BF16), ones_bd) * v
    r_ref, lw_ref, k_ref, v_ref, a_ref, b_ref, g_ref, bonus_ref = out_refs
    r_ref[...] = r
    lw_ref[...] = log_decay
    k_ref[...] = k2
    v_ref[...] = v
    a_ref[...] = -kk
    b_ref[...] = kk * asig
    g_ref[...] = g
    bonus_ref[...] = bonus


def _prep_carry_kernel(p_ref, mu, w0, a0, kkw, kaw, rkw, wwa, wg, ones_bd, *rest):
    out_refs, carry = rest[:8], rest[8]
    i = pl.program_id(1)

    @pl.when(i == 0)
    def _():
        carry[...] = jnp.zeros(carry.shape, F32)

    p = p_ref[...]
    row = lax.broadcasted_iota(jnp.int32, p.shape, 0)
    ps = jnp.where(row == 0, carry[...], pltpu.roll(p, 1, 0))
    carry[...] = p[p.shape[0] - 1:, :]
    _prep_math(p, ps, mu[...], w0[...], a0[...], kkw[...], kaw[...], rkw[...], wwa[...], wg[...],
               ones_bd[...], out_refs)


def _prep_shift_kernel(p_ref, ps_ref, mu, w0, a0, kkw, kaw, rkw, wwa, wg, ones_bd, *out_refs):
    _prep_math(p_ref[...], ps_ref[...], mu[...], w0[...], a0[...], kkw[...], kaw[...], rkw[...], wwa[...],
               wg[...], ones_bd[...], out_refs)


def _rwkv_prep(p, p_shift, consts, n_seq, tm):
    m = p.shape[0]
    per_seq = m // n_seq // tm if p_shift is None else 0
    if p_shift is None:
        grid = (n_seq, per_seq)
        tile = lambda w: pl.BlockSpec((tm, w), lambda s, i: (s * per_seq + i, 0))
        const = lambda a: pl.BlockSpec(a.shape, lambda s, i: (0,) * a.ndim)
        kern, args, sem = _prep_carry_kernel, (p,), ("arbitrary", "arbitrary")
        scratch = [pltpu.VMEM((1, P_PAD), F32)]
    else:
        grid = (m // tm,)
        tile = lambda w: pl.BlockSpec((tm, w), lambda i: (i, 0))
        const = lambda a: pl.BlockSpec(a.shape, lambda i: (0,) * a.ndim)
        kern, args, sem = _prep_shift_kernel, (p, p_shift), ("arbitrary",)
        scratch = []
    return pl.pallas_call(
        kern,
        grid=grid,
        in_specs=[tile(P_PAD)] * len(args) + [const(a) for a in consts],
        out_specs=[tile(RWKV_WIDTH)] * 8,
        out_shape=[jax.ShapeDtypeStruct((m, RWKV_WIDTH), F32)] * 8,
        scratch_shapes=scratch,
        compiler_params=_params(*sem),
        name="rwkv_prep",
    )(*args, *consts)


def _wkv_kernel(r_ref, lw_ref, k_ref, v_ref, a_ref, b_ref, s0_ref, y_ref, sout_ref, s_scr):
    c = pl.program_id(1)
    nc = pl.num_programs(1)
    C = CHUNK
    pairs = range(RWKV_PAIRS)

    @pl.when(c == 0)
    def _():
        s_scr[...] = s0_ref[0]

    lw = lw_ref[0]
    trow = lax.broadcasted_iota(jnp.int32, (C, C), 0)
    tcol = lax.broadcasted_iota(jnp.int32, (C, C), 1)
    tri = jnp.where(tcol <= trow, 1.0, 0.0).astype(BF16)
    h1 = lw.astype(BF16)
    r1 = lw - h1.astype(F32)
    h2 = r1.astype(BF16)
    h3 = (r1 - h2.astype(F32)).astype(BF16)
    cs = _dot(tri, jnp.concatenate([h1, h2, h3], axis=1))
    cum = cs[:, :RWKV_WIDTH] + cs[:, RWKV_WIDTH:2 * RWKV_WIDTH] + cs[:, 2 * RWKV_WIDTH:]
    tot = cum[C - 1:C, :]
    e_inv = jnp.exp(-cum)
    e_tail = jnp.exp(tot - cum)
    at_all = a_ref[0] * jnp.exp(cum - lw)
    rt_all = r_ref[0] * jnp.exp(cum)
    bt_all = b_ref[0] * e_inv
    kt_all = k_ref[0] * e_inv
    bh_all = b_ref[0] * e_tail
    kh_all = k_ref[0] * e_tail
    w_tot = jnp.exp(tot)
    v_all = v_ref[0]

    row = lax.broadcasted_iota(jnp.int32, (2 * C, 2 * C), 0)
    col = lax.broadcasted_iota(jnp.int32, (2 * C, 2 * C), 1)
    same = (row // C) == (col // C)
    strict = same & (col < row)
    incl = same & (col <= row)
    eye = jnp.where(row == col, 1.0, 0.0)
    lane = lax.broadcasted_iota(jnp.int32, (C, 128), 1)
    first = lane < RWKV_HEAD
    bf = lambda x: x.astype(BF16)
    ls = [slice(pi * 128, (pi + 1) * 128) for pi in pairs]

    def stack(x):
        return jnp.concatenate([jnp.where(first, x, 0.0), jnp.where(first, 0.0, x)], axis=0)

    def dup(x):
        return jnp.concatenate([x, x], axis=0)

    at_s = [stack(at_all[:, s]) for s in ls]
    rt_s = [stack(rt_all[:, s]) for s in ls]
    v_s = [stack(v_all[:, s]) for s in ls]
    g = [_dot_nt(bf(jnp.concatenate([at_s[i], rt_s[i]], axis=0)),
                 bf(jnp.concatenate([dup(bt_all[:, ls[i]]), dup(kt_all[:, ls[i]])], axis=0))) for i in pairs]
    l_ab = [jnp.where(strict, g[i][0:2 * C, 0:2 * C], 0.0) for i in pairs]
    a_ak = [jnp.where(strict, g[i][0:2 * C, 2 * C:4 * C], 0.0) for i in pairs]
    a_rb = [jnp.where(incl, g[i][2 * C:4 * C, 0:2 * C], 0.0) for i in pairs]
    a_rk = [jnp.where(incl, g[i][2 * C:4 * C, 2 * C:4 * C], 0.0) for i in pairs]
    x = [eye + l for l in l_ab]
    pw = l_ab
    for _ in range(int(math.log2(C)) - 1):
        pw = [_dot(bf(m), bf(m)) for m in pw]
        x = [x[i] + _dot(bf(x[i]), bf(pw[i])) for i in pairs]
    av = [_dot(bf(jnp.concatenate([a_ak[i], a_rk[i]], axis=0)), bf(v_s[i])) for i in pairs]
    tx = [_dot(bf(x[i]), bf(jnp.concatenate([at_s[i], av[i][0:2 * C]], axis=1))) for i in pairs]
    s_old = [s_scr[i] for i in pairs]
    az = [_dot_nt(bf(jnp.concatenate([tx[i][:, 0:128], rt_s[i]], axis=0)), bf(s_old[i])) for i in pairs]
    u = [az[i][0:2 * C] + tx[i][:, 128:256] for i in pairs]
    y = [az[i][2 * C:4 * C] + _dot(bf(a_rb[i]), bf(u[i])) + av[i][2 * C:4 * C] for i in pairs]
    for i in pairs:
        y_ref[0, :, ls[i]] = y[i][0:C] + y[i][C:2 * C]
    upd = [_dot_tn(bf(jnp.concatenate([u[i], v_s[i]], axis=0)),
                   bf(jnp.concatenate([stack(bh_all[:, ls[i]]), stack(kh_all[:, ls[i]])], axis=0))) for i in pairs]
    for i in pairs:
        s_scr[i] = s_old[i] * w_tot[:, ls[i]] + upd[i]

    @pl.when(c == nc - 1)
    def _():
        sout_ref[0] = s_scr[...]


def _wkv(r, lw, k, v, a, b, s0_bd):
    n_seq, t, _ = r.shape
    seq_spec = pl.BlockSpec((1, CHUNK, RWKV_WIDTH), lambda s, c: (s, c, 0))
    st_spec = pl.BlockSpec((1, RWKV_PAIRS, 128, 128), lambda s, c: (s, 0, 0, 0))
    return pl.pallas_call(
        _wkv_kernel,
        grid=(n_seq, t // CHUNK),
        in_specs=[seq_spec] * 6 + [st_spec],
        out_specs=[seq_spec, st_spec],
        out_shape=[jax.ShapeDtypeStruct((n_seq, t, RWKV_WIDTH), F32),
                   jax.ShapeDtypeStruct((n_seq, RWKV_PAIRS, 128, 128), F32)],
        scratch_shapes=[pltpu.VMEM((RWKV_PAIRS, 128, 128), F32)],
        compiler_params=_params("arbitrary", "arbitrary"),
        name="wkv_scan",
    )(r, lw, k, v, a, b, s0_bd)


def _outproj_kernel(att_ref, y_ref, g_ref, bonus_ref, lng_ref, lnb_ref, ones_ref, wo_ref, x_ref, mod_ref,
                    gpost_ref, o_ref):
    y = y_ref[...]
    ones_bd = ones_ref[...]
    inv_n = 1.0 / RWKV_HEAD
    yc = y - _dot(y.astype(BF16), ones_bd) * inv_n
    var = _dot((yc * yc).astype(BF16), ones_bd) * inv_n
    yn = yc * lax.rsqrt(var + LNX_EPS)
    rw = (yn * lng_ref[...] + lnb_ref[...] + bonus_ref[...]) * g_ref[...]
    mix = (_dot(att_ref[...].astype(BF16), wo_ref[0:ATT_WIDTH, :])
           + _dot(rw.astype(BF16), wo_ref[ATT_WIDTH:, :]))
    o_ref[...] = x_ref[...] + mod_ref[0, 2] * _rms(mix, gpost_ref[...], RMS_EPS)


def _outproj(att, y, g, bonus, lng, lnb, ones_bd, w_out_b, x, mod4, g_post, tm, tiles_per_mod):
    m = x.shape[0]
    r = mod4.shape[2]
    tile = lambda w: pl.BlockSpec((tm, w), lambda i: (i, 0))
    const = lambda a: pl.BlockSpec(a.shape, lambda i: (0,) * a.ndim)
    return pl.pallas_call(
        _outproj_kernel,
        grid=(m // tm,),
        in_specs=[tile(ATT_WIDTH)] * 4 + [const(lng), const(lnb), const(ones_bd), const(w_out_b),
                                          tile(D_MODEL),
                                          pl.BlockSpec((1, 6, r, D_MODEL), lambda i: (i // tiles_per_mod, 0, 0, 0)),
                                          const(g_post)],
        out_specs=tile(D_MODEL),
        out_shape=jax.ShapeDtypeStruct((m, D_MODEL), F32),
        compiler_params=_params("arbitrary"),
        name="out_proj",
    )(att, y, g, bonus, lng, lnb, ones_bd, w_out_b, x, mod4, g_post)


FFN_TF = 1024


def _ffn_kernel(x_ref, mod_ref, gpre_ref, gpost_ref, wu_ref, wd_ref, o_ref, h_scr, acc_scr):
    f = pl.program_id(1)

    @pl.when(f == 0)
    def _():
        h = _rms(x_ref[...], gpre_ref[...], RMS_EPS) * (1.0 + mod_ref[0, 4]) + mod_ref[0, 3]
        h_scr[...] = h.astype(BF16)
        acc_scr[...] = jnp.zeros(acc_scr.shape, F32)

    u = jnp.maximum(_dot(h_scr[...], wu_ref[...]), 0.0)
    acc_scr[...] += _dot((u * u).astype(BF16), wd_ref[...])

    @pl.when(f == pl.num_programs(1) - 1)
    def _():
        o_ref[...] = x_ref[...] + mod_ref[0, 5] * _rms(acc_scr[...], gpost_ref[...], RMS_EPS)


def _ffn(x, mod4, g_pre, g_post, w_up_b, w_down_b, tm, tiles_per_mod):
    m = x.shape[0]
    r = mod4.shape[2]
    return pl.pallas_call(
        _ffn_kernel,
        grid=(m // tm, FFN_DIM // FFN_TF),
        in_specs=[pl.BlockSpec((tm, D_MODEL), lambda i, f: (i, 0)),
                  pl.BlockSpec((1, 6, r, D_MODEL), lambda i, f: (i // tiles_per_mod, 0, 0, 0)),
                  pl.BlockSpec((1, D_MODEL), lambda i, f: (0, 0)),
                  pl.BlockSpec((1, D_MODEL), lambda i, f: (0, 0)),
                  pl.BlockSpec((D_MODEL, FFN_TF), lambda i, f: (0, f)),
                  pl.BlockSpec((FFN_TF, D_MODEL), lambda i, f: (f, 0))],
        out_specs=pl.BlockSpec((tm, D_MODEL), lambda i, f: (i, 0)),
        out_shape=jax.ShapeDtypeStruct((m, D_MODEL), F32),
        scratch_shapes=[pltpu.VMEM((tm, D_MODEL), BF16), pltpu.VMEM((tm, D_MODEL), F32)],
        compiler_params=_params("arbitrary", "arbitrary"),
        name="ffn",
    )(x, mod4, g_pre, g_post, w_up_b, w_down_b)


def _state_to_blockdiag(s):
    n = s.shape[0]
    s = s.reshape(n, RWKV_PAIRS, 2, RWKV_HEAD, RWKV_HEAD)
    z = jnp.zeros((n, RWKV_PAIRS, RWKV_HEAD, RWKV_HEAD), s.dtype)
    top = jnp.concatenate([s[:, :, 0], z], axis=-1)
    bot = jnp.concatenate([z, s[:, :, 1]], axis=-1)
    return jnp.concatenate([top, bot], axis=-2)


def _blockdiag_to_state(sbd):
    n = sbd.shape[0]
    s = jnp.stack([sbd[:, :, :RWKV_HEAD, :RWKV_HEAD], sbd[:, :, RWKV_HEAD:, RWKV_HEAD:]], axis=2)
    return s.reshape(n, 2 * RWKV_PAIRS, RWKV_HEAD, RWKV_HEAD)


def _pad_cols(a, width):
    return jnp.pad(a, ((0, 0), (0, width - a.shape[1])))


def kernel(x_prompt, x_sample, cache_k, cache_v, state_wkv, state_shift, page_table, c_prompt, c_sample, bias_table, w_ada, b_ada, g_pre_mix, g_post_mix, g_pre_ffn, g_post_ffn, w_in, mu_shift, w0, w_lora_w, a0, w_lora_a, w_lora_g, k_k, k_a, r_k, lnx_g, lnx_b, lam_q1, lam_k1, lam_q2, lam_k2, subln_g, w_out, w_ffn_up, w_ffn_down):
    bsz, seq, d = x_prompt.shape
    dbs, tdec, _ = x_sample.shape
    n_p, n_s = bsz * seq, dbs * tdec

    w_in_b = jnp.concatenate([w_in[0][:, 3 * ATT_WIDTH:], jnp.zeros((d, P_PAD - RWKV_PROJ), F32),
                              w_in[0][:, :3 * ATT_WIDTH]], axis=1).astype(BF16)
    w_out_b = w_out[0].astype(BF16)
    w_up_b = w_ffn_up[0].astype(BF16)
    w_down_b = w_ffn_down[0].astype(BF16)
    wwa = jnp.zeros((128, 2 * RWKV_WIDTH), F32)
    wwa = wwa.at[:64, :RWKV_WIDTH].set(w_lora_w[0]).at[64:, RWKV_WIDTH:].set(w_lora_a[0]).astype(BF16)
    wg = jnp.pad(w_lora_g[0], ((0, P_PAD - GATE_IN - w_lora_g.shape[1]), (0, 0))).astype(BF16)
    head_id = jnp.arange(RWKV_WIDTH) // RWKV_HEAD
    ones_bd = (head_id[:, None] == head_id[None, :]).astype(BF16)
    prep_consts = (_pad_cols(mu_shift, P_PAD), w0, a0, k_k, k_a, r_k.reshape(1, RWKV_WIDTH), wwa, wg, ones_bd)
    lamv = jnp.concatenate([lam_q1, lam_k1, lam_q2, lam_k2], axis=0)

    n_c = bsz + dbs
    c_all = jnp.pad(jnp.concatenate([c_prompt, c_sample], axis=0), ((0, (-n_c) % 8), (0, 0)))
    mod = _ada(c_all, w_ada[0], b_ada)
    mod_p = mod[:bsz].reshape(bsz, 6, 1, d)
    mod_s = jnp.repeat(mod[bsz:n_c].reshape(dbs, 6, d), tdec, axis=0).transpose(1, 0, 2).reshape(1, 6, n_s, d)

    xp = x_prompt.reshape(n_p, d)
    xs = x_sample.reshape(n_s, d)
    tm_p = min(1024, seq)
    proj_p, qkvb_p = _inproj(xp, mod_p, g_pre_mix, w_in_b, tm_p, seq // tm_p)
    proj_s, qkvb_s = _inproj(xs, mod_s, g_pre_mix, w_in_b, n_s, 1)
    k_col, v_col = P_PAD + ATT_WIDTH, P_PAD + 2 * ATT_WIDTH
    k_p, v_p = proj_p[:, k_col:v_col], proj_p[:, v_col:]
    k_s, v_s, p_s = proj_s[:, k_col:v_col], proj_s[:, v_col:], proj_s[:, :P_PAD]

    att_p = _prompt_attn(qkvb_p.reshape(bsz, seq, 3 * ATT_WIDTH), bias_table, lamv, subln_g)
    q4 = qkvb_s[:, :ATT_WIDTH].reshape(dbs, tdec, ATT_HEADS, HEAD_V).transpose(0, 2, 1, 3)
    first = jnp.arange(HEAD_V) < HEAD_QK
    zq = jnp.zeros((dbs, ATT_HEADS, 8 - tdec, HEAD_V), BF16)
    q_rows = jnp.concatenate([jnp.where(first, q4, 0), zq, jnp.where(first, 0, q4), zq], axis=2)
    n_pool = cache_k.shape[1]
    att_s = _sample_attn(q_rows.reshape(dbs, ATT_HEADS * QROWS, HEAD_V),
                         k_s.reshape(dbs, tdec, ATT_WIDTH), v_s.reshape(dbs, tdec, ATT_WIDTH),
                         cache_k[0].reshape(n_pool, PAGE_ROWS, HEAD_V), cache_v[0].reshape(n_pool, PAGE_ROWS, HEAD_V),
                         page_table, bias_table, lamv, subln_g)

    pre_p = _rwkv_prep(proj_p, None, prep_consts, bsz, min(256, seq))
    p_s3 = p_s.reshape(dbs, tdec, P_PAD)
    shift_s = jnp.concatenate([_pad_cols(state_shift[0], P_PAD)[:, None, :], p_s3[:, :-1]], axis=1)
    pre_s = _rwkv_prep(p_s, shift_s.reshape(n_s, P_PAD), prep_consts, dbs, n_s)
    r_p, lw_p, kk_p, vv_p, a_p, b_p, g_p, bonus_p = pre_p
    r_s, lw_s, kk_s, vv_s, a_s, b_s, g_s, bonus_s = pre_s
    seq3 = lambda t: t.reshape(bsz, seq, RWKV_WIDTH)
    y_p, sbd_p = _wkv(seq3(r_p), seq3(lw_p), seq3(kk_p), seq3(vv_p), seq3(a_p), seq3(b_p),
                      jnp.zeros((bsz, RWKV_PAIRS, 128, 128), F32))
    dec3 = lambda t: jnp.pad(t.reshape(dbs, tdec, RWKV_WIDTH), ((0, 0), (0, CHUNK - tdec), (0, 0)))
    y_s, sbd_s = _wkv(dec3(r_s), dec3(lw_s), dec3(kk_s), dec3(vv_s), dec3(a_s), dec3(b_s),
                      _state_to_blockdiag(state_wkv[0]))
    y_s = y_s[:, :tdec].reshape(n_s, RWKV_WIDTH)

    tm_o = min(256, seq)
    x1_p = _outproj(att_p.reshape(n_p, ATT_WIDTH), y_p.reshape(n_p, RWKV_WIDTH), g_p, bonus_p, lnx_g, lnx_b,
                    ones_bd, w_out_b, xp, mod_p, g_post_mix, tm_o, seq // tm_o)
    x1_s = _outproj(att_s.reshape(n_s, ATT_WIDTH), y_s, g_s, bonus_s, lnx_g, lnx_b,
                    ones_bd, w_out_b, xs, mod_s, g_post_mix, n_s, 1)
    tm_f = min(512, seq)
    out_p = _ffn(x1_p, mod_p, g_pre_ffn, g_post_ffn, w_up_b, w_down_b, tm_f, seq // tm_f)
    out_s = _ffn(x1_s, mod_s, g_pre_ffn, g_post_ffn, w_up_b, w_down_b, n_s, 1)

    return (out_p.reshape(bsz, seq, d),
            out_s.reshape(dbs, tdec, d),
            k_p.reshape(1, bsz, seq, ATT_HEADS, HEAD_V),
            v_p.reshape(1, bsz, seq, ATT_HEADS, HEAD_V),
            _blockdiag_to_state(sbd_p)[None],
            proj_p.reshape(bsz, seq, PROJ_W)[None, :, -1, :RWKV_PROJ],
            k_s.reshape(1, dbs, tdec, ATT_HEADS, HEAD_V),
            v_s.reshape(1, dbs, tdec, ATT_HEADS, HEAD_V),
            _blockdiag_to_state(sbd_s)[None],
            p_s3[None, :, -1, :RWKV_PROJ])
```

```python
import functools
import math

import numpy as np
import jax
import jax.numpy as jnp
from jax import lax
from jax.experimental import pallas as pl
from jax.experimental.pallas import tpu as pltpu

F32 = jnp.float32
BF16 = jnp.bfloat16

D_MODEL = 2048
ATT_WIDTH = 1024
RWKV_WIDTH = 1024
ATT_HEADS = 8
HEAD_V = 128
HEAD_QK = 64
RWKV_HEAD = 64
RWKV_PAIRS = RWKV_WIDTH // 128
RWKV_PROJ = 3360
P_PAD = 3584
LORA_IN = 3072
GATE_IN = 3200
FFN_DIM = 8192
N_BUCKETS = 32
MAX_DISTANCE = 128
PAGE = 128
ATT_SCALE = HEAD_QK ** -0.5
RMS_EPS = 1e-6
SUBLN_EPS = 1e-5
LNX_EPS = 64e-5
NEG_INF = -1e30
LAMBDA_INIT = 0.8 - 0.6 * math.exp(-0.3 * 0)
CHUNK = 64
VMEM_LIMIT = 56 * 1024 * 1024


def _params(*sem):
    return pltpu.CompilerParams(dimension_semantics=sem, vmem_limit_bytes=VMEM_LIMIT)


def _dot(a, b):
    return jnp.dot(a, b, preferred_element_type=F32)


def _dot_nt(a, b):
    return lax.dot_general(a, b, (((1,), (1,)), ((), ())), preferred_element_type=F32)


def _dot_tn(a, b):
    return lax.dot_general(a, b, (((0,), (0,)), ((), ())), preferred_element_type=F32)


def _t5_bucket_np(dist):
    max_exact = N_BUCKETS // 2
    d = np.maximum(dist, 0)
    ratio = np.log(np.maximum(d, 1).astype(np.float32) / max_exact) / math.log(MAX_DISTANCE / max_exact)
    large = np.minimum(max_exact + (ratio * (N_BUCKETS - max_exact)).astype(np.int32), N_BUCKETS - 1)
    return np.where(d < max_exact, d, large).astype(np.int32)


def _table_lookup(bucket, tbl_ref, h):
    out = jnp.zeros(bucket.shape, F32)
    for b in range(N_BUCKETS):
        out = jnp.where(bucket == b, tbl_ref[b, h], out)
    return out


def _lam(lamv_ref):
    v = lamv_ref[...]
    s1 = jnp.sum(v[0:1] * v[1:2], axis=-1, keepdims=True)
    s2 = jnp.sum(v[2:3] * v[3:4], axis=-1, keepdims=True)
    return jnp.exp(s1) - jnp.exp(s2) + LAMBDA_INIT


def _rms(x, g, eps):
    return x * lax.rsqrt(jnp.mean(x * x, axis=-1, keepdims=True) + eps) * g


def _lanes(x, reps):
    return jnp.concatenate([x] * reps, axis=1)


def _ada_kernel(c_ref, w_ref, b_ref, o_ref):
    c = c_ref[...]
    x = (c * jax.nn.sigmoid(c)).astype(BF16)
    o_ref[...] = _dot(x, w_ref[...].astype(BF16)) + b_ref[...]


def _ada(c_all, w_ada, b_ada):
    rows, n = c_all.shape[0], w_ada.shape[1]
    tn = 1536
    return pl.pallas_call(
        _ada_kernel,
        grid=(n // tn,),
        in_specs=[pl.BlockSpec((rows, D_MODEL), lambda j: (0, 0)),
                  pl.BlockSpec((D_MODEL, tn), lambda j: (0, j)),
                  pl.BlockSpec((1, tn), lambda j: (0, j))],
        out_specs=pl.BlockSpec((rows, tn), lambda j: (0, j)),
        out_shape=jax.ShapeDtypeStruct((rows, n), F32),
        compiler_params=_params("arbitrary"),
        name="ada_mod",
    )(c_all, w_ada, b_ada)


IN_TN = 512


PROJ_W = P_PAD + 3 * ATT_WIDTH
NPJ = P_PAD // IN_TN


def _inproj_kernel(x_ref, mod_ref, g_ref, w_ref, proj_ref, qkvb_ref, h_scr):
    j = pl.program_id(1)

    @pl.when(j == 0)
    def _():
        h = _rms(x_ref[...], g_ref[...], RMS_EPS) * (1.0 + mod_ref[0, 1]) + mod_ref[0, 0]
        h_scr[...] = h.astype(BF16)

    proj_ref[...] = _dot(h_scr[...], w_ref[...])

    @pl.when(j >= NPJ)
    def _():
        scale = jnp.where(j < NPJ + ATT_WIDTH // IN_TN, ATT_SCALE, 1.0)
        qkvb_ref[...] = (proj_ref[...] * scale).astype(BF16)


def _inproj(x, mod4, g_pre, w_in_b, tm, tiles_per_mod):
    m = x.shape[0]
    r = mod4.shape[2]
    nj = PROJ_W // IN_TN
    return pl.pallas_call(
        _inproj_kernel,
        grid=(m // tm, nj),
        in_specs=[pl.BlockSpec((tm, D_MODEL), lambda i, j: (i, 0)),
                  pl.BlockSpec((1, 6, r, D_MODEL), lambda i, j: (i // tiles_per_mod, 0, 0, 0)),
                  pl.BlockSpec((1, D_MODEL), lambda i, j: (0, 0)),
                  pl.BlockSpec((D_MODEL, IN_TN), lambda i, j: (0, j))],
        out_specs=[pl.BlockSpec((tm, IN_TN), lambda i, j: (i, j)),
                   pl.BlockSpec((tm, IN_TN), lambda i, j: (i, jnp.maximum(j - NPJ, 0)))],
        out_shape=[jax.ShapeDtypeStruct((m, PROJ_W), F32),
                   jax.ShapeDtypeStruct((m, 3 * ATT_WIDTH), BF16)],
        scratch_shapes=[pltpu.VMEM((tm, D_MODEL), BF16)],
        compiler_params=_params("arbitrary", "arbitrary"),
        name="in_proj",
    )(x, mod4, g_pre, w_in_b)


ATT_T = 512
ATT_SUB = 128


ATT_HPS = 4


def _pattn_kernel(qi_ref, ki_ref, tbl_ref, lamv_ref, bkt_ref, subg_ref, q_ref, k_ref, v_ref, o_ref,
                  bias_scr, q2_scr, m_scr, l_scr, acc_scr):
    hg = pl.program_id(1)
    step = pl.program_id(2)
    qi = qi_ref[step]
    ki = ki_ref[step]
    nsub = ATT_T // ATT_SUB
    heads = range(ATT_HPS)
    lanes = [slice(u * HEAD_V, (u + 1) * HEAD_V) for u in heads]

    @pl.when(step == 0)
    def _():
        row = lax.broadcasted_iota(jnp.int32, (ATT_SUB, ATT_SUB), 0)
        col = lax.broadcasted_iota(jnp.int32, (ATT_SUB, ATT_SUB), 1)
        for u in heads:
            h = hg * ATT_HPS + u
            t0 = jnp.where(col > row, NEG_INF, _table_lookup(bkt_ref[0], tbl_ref, h))
            t1 = _table_lookup(bkt_ref[1], tbl_ref, h)
            far = jnp.full((ATT_SUB, ATT_SUB), tbl_ref[N_BUCKETS - 1, h], F32)
            masked = jnp.full((ATT_SUB, ATT_SUB), NEG_INF, F32)
            for rb in range(nsub):
                for cb in range(nsub):
                    d = rb - cb
                    diag = t0 if d == 0 else t1 if d == 1 else far if d >= 2 else masked
                    off = t1 if (rb == 0 and cb == nsub - 1) else far
                    rs, cs = slice(rb * ATT_SUB, (rb + 1) * ATT_SUB), slice(cb * ATT_SUB, (cb + 1) * ATT_SUB)
                    bias_scr[u, 0, rs, cs] = diag
                    bias_scr[u, 1, rs, cs] = off
                    bias_scr[u, 2, rs, cs] = far

    @pl.when(ki == 0)
    def _():
        lane = lax.broadcasted_iota(jnp.int32, (ATT_T, HEAD_V), 1)
        for u in heads:
            q = q_ref[0, :, lanes[u]]
            zero = jnp.zeros_like(q)
            q2_scr[u, 0:ATT_T, :] = jnp.where(lane < HEAD_QK, q, zero)
            q2_scr[u, ATT_T:, :] = jnp.where(lane >= HEAD_QK, q, zero)
        m_scr[...] = jnp.full(m_scr.shape, NEG_INF, F32)
        l_scr[...] = jnp.zeros(l_scr.shape, F32)
        acc_scr[...] = jnp.zeros(acc_scr.shape, F32)

    which = jnp.minimum(qi - ki, 2)
    s = []
    for u in heads:
        bias = bias_scr[u, which]
        s.append(_dot_nt(q2_scr[u], k_ref[0, :, lanes[u]]) + jnp.concatenate([bias, bias], axis=0))
    m_prev = [m_scr[u] for u in heads]
    m_new = [jnp.maximum(m_prev[u], jnp.max(s[u], axis=-1, keepdims=True)) for u in heads]
    alpha = [jnp.exp(m_prev[u] - m_new[u]) for u in heads]
    p = [jnp.exp(s[u] - _lanes(m_new[u], ATT_T // 128)) for u in heads]
    for u in heads:
        l_scr[u] = alpha[u] * l_scr[u] + jnp.sum(p[u], axis=-1, keepdims=True)
        acc_scr[u] = alpha[u] * acc_scr[u] + _dot(p[u].astype(BF16), v_ref[0, :, lanes[u]])
        m_scr[u] = m_new[u]

    @pl.when(ki == qi)
    def _():
        lam = _lam(lamv_ref)
        for u in heads:
            on = acc_scr[u] / l_scr[u]
            o = on[0:ATT_T] - lam * on[ATT_T:]
            o = _rms(o, subg_ref[...], SUBLN_EPS) * (1.0 - LAMBDA_INIT)
            o_ref[0, :, lanes[u]] = o.astype(BF16)


def _prompt_attn(qkv, tbl, lamv, subg):
    b, s, _ = qkv.shape
    nq = s // ATT_T
    r = np.arange(ATT_SUB)
    dist = r[:, None] - r[None, :]
    bkt = jnp.asarray(np.stack([_t5_bucket_np(dist), _t5_bucket_np(dist + ATT_SUB)]))
    pairs = [(qi, ki) for qi in range(nq) for ki in range(qi + 1)]
    qi_of = jnp.asarray(np.array([p[0] for p in pairs], np.int32))
    ki_of = jnp.asarray(np.array([p[1] for p in pairs], np.int32))
    groups = ATT_HEADS // ATT_HPS
    width = ATT_HPS * HEAD_V
    q_spec = pl.BlockSpec((1, ATT_T, width), lambda bi, h, st, qi, ki: (bi, qi[st], h))
    k_spec = pl.BlockSpec((1, ATT_T, width), lambda bi, h, st, qi, ki: (bi, ki[st], groups + h))
    v_spec = pl.BlockSpec((1, ATT_T, width), lambda bi, h, st, qi, ki: (bi, ki[st], 2 * groups + h))
    const = lambda shape: pl.BlockSpec(shape, lambda bi, h, st, qi, ki: (0,) * len(shape))
    grid_spec = pltpu.PrefetchScalarGridSpec(
        num_scalar_prefetch=2,
        grid=(b, groups, len(pairs)),
        in_specs=[pl.BlockSpec(memory_space=pltpu.SMEM), const((4, HEAD_QK)), const((2, ATT_SUB, ATT_SUB)),
                  const((1, HEAD_V)), q_spec, k_spec, v_spec],
        out_specs=q_spec,
        scratch_shapes=[pltpu.VMEM((ATT_HPS, 3, ATT_T, ATT_T), F32),
                        pltpu.VMEM((ATT_HPS, 2 * ATT_T, HEAD_V), BF16),
                        pltpu.VMEM((ATT_HPS, 2 * ATT_T, 128), F32),
                        pltpu.VMEM((ATT_HPS, 2 * ATT_T, 128), F32),
                        pltpu.VMEM((ATT_HPS, 2 * ATT_T, HEAD_V), F32)],
    )
    return pl.pallas_call(
        _pattn_kernel,
        grid_spec=grid_spec,
        out_shape=jax.ShapeDtypeStruct((b, s, ATT_WIDTH), BF16),
        compiler_params=_params("arbitrary", "arbitrary", "arbitrary"),
        name="prompt_attn",
    )(qi_of, ki_of, tbl, lamv, bkt, subg, qkv, qkv, qkv)


PAGES_PER_STEP = 16
QROWS = 16
PAGE_ROWS = PAGE * ATT_HEADS


def _sattn_kernel(pt_ref, tbl_ref, lamv_ref, bkt_ref, subg_ref, q_ref, kn_ref, vn_ref, *rest):
    g_pages = PAGES_PER_STEP
    k_refs = rest[:g_pages]
    v_refs = rest[g_pages:2 * g_pages]
    o_ref, bias_scr, m_scr, l_scr, acc_scr = rest[2 * g_pages:]
    del pt_ref
    b = pl.program_id(0)
    j = pl.program_id(1)
    nj = pl.num_programs(1)
    n_new = kn_ref.shape[1]
    nrow = ATT_HEADS * QROWS

    @pl.when((b == 0) & (j == 0))
    def _():
        key_head = lax.broadcasted_iota(jnp.int32, (QROWS, PAGE_ROWS), 1) % ATT_HEADS
        for h in range(ATT_HEADS):
            rows = slice(h * QROWS, (h + 1) * QROWS)
            bias_scr[0, rows, :] = jnp.where(key_head == h, tbl_ref[N_BUCKETS - 1, h], NEG_INF)
            bias_scr[1, rows, :] = jnp.where(key_head == h, _table_lookup(bkt_ref[...], tbl_ref, h), NEG_INF)

    @pl.when(j == 0)
    def _():
        m_scr[...] = jnp.full(m_scr.shape, NEG_INF, F32)
        l_scr[...] = jnp.zeros(l_scr.shape, F32)
        acc_scr[...] = jnp.zeros(acc_scr.shape, F32)

    q = q_ref[0]
    s_pages = []
    for g in range(g_pages):
        bias = bias_scr[(j == nj - 1).astype(jnp.int32)] if g == g_pages - 1 else bias_scr[0]
        s_pages.append(_dot_nt(q, k_refs[g][0].astype(BF16)) + bias)
    m_run = m_scr[...]
    m_new = m_run
    for s in s_pages:
        m_new = jnp.maximum(m_new, jnp.max(s, axis=-1, keepdims=True))
    alpha = jnp.exp(m_run - m_new)
    m_lanes = _lanes(m_new, PAGE_ROWS // 128)
    l_run = alpha * l_scr[...]
    acc = alpha * acc_scr[...]
    for g in range(g_pages):
        p = jnp.exp(s_pages[g] - m_lanes)
        l_run = l_run + jnp.sum(p, axis=-1, keepdims=True)
        acc = acc + _dot(p.astype(BF16), v_refs[g][0].astype(BF16))
    m_run = m_new
    m_scr[...] = m_run
    l_scr[...] = l_run
    acc_scr[...] = acc

    @pl.when(j == nj - 1)
    def _():
        lam = _lam(lamv_ref)
        qf = q.astype(F32)
        t_row = lax.broadcasted_iota(jnp.int32, (nrow, 1), 0) % 8
        head_of_row = lax.broadcasted_iota(jnp.int32, (nrow, 1), 0) // QROWS
        m_fin, l_fin, acc_fin = m_run, l_run, acc
        tbl_rows = []
        for dd in range(n_new):
            tbl_row = jnp.zeros((nrow, 1), F32)
            for h in range(ATT_HEADS):
                tbl_row = jnp.where(head_of_row == h, tbl_ref[dd, h], tbl_row)
            tbl_rows.append(tbl_row)
        for tn in range(n_new):
            d = t_row - tn
            bias = jnp.full((nrow, 1), NEG_INF, F32)
            for dd in range(n_new):
                bias = jnp.where(d == dd, tbl_rows[dd], bias)
            k_rows = jnp.concatenate(
                [jnp.broadcast_to(kn_ref[0, tn:tn + 1, h * HEAD_V:(h + 1) * HEAD_V], (QROWS, HEAD_V))
                 for h in range(ATT_HEADS)], axis=0)
            v_rows = jnp.concatenate(
                [jnp.broadcast_to(vn_ref[0, tn:tn + 1, h * HEAD_V:(h + 1) * HEAD_V], (QROWS, HEAD_V))
                 for h in range(ATT_HEADS)], axis=0)
            s = jnp.sum(qf * k_rows, axis=-1, keepdims=True) + bias
            m_new = jnp.maximum(m_fin, s)
            alpha = jnp.exp(m_fin - m_new)
            p = jnp.exp(s - m_new)
            l_fin = alpha * l_fin + p
            acc_fin = alpha * acc_fin + p * v_rows
            m_fin = m_new
        on = acc_fin / l_fin
        for h in range(ATT_HEADS):
            o = on[h * QROWS:h * QROWS + 8] - lam * on[h * QROWS + 8:(h + 1) * QROWS]
            o = _rms(o, subg_ref[...], SUBLN_EPS) * (1.0 - LAMBDA_INIT)
            o_ref[0, :, h * HEAD_V:(h + 1) * HEAD_V] = o[0:n_new]


def _sample_attn(q_rows, k_new, v_new, cache_k, cache_v, page_table, tbl, lamv, subg):
    db, t_new = k_new.shape[0], k_new.shape[1]
    n_pages = page_table.shape[1]
    g_pages = PAGES_PER_STEP
    nrow = ATT_HEADS * QROWS
    t_row = (np.arange(QROWS) % 8)[:, None]
    tok = (np.arange(PAGE_ROWS) // ATT_HEADS)[None, :]
    bkt = jnp.asarray(_t5_bucket_np(PAGE + t_row - tok))

    def page_spec(g):
        return pl.BlockSpec((1, PAGE_ROWS, HEAD_V), lambda b, j, pt: (pt[b, j * g_pages + g], 0, 0))

    new_spec = pl.BlockSpec((1, t_new, ATT_WIDTH), lambda b, j, pt: (b, 0, 0))
    grid_spec = pltpu.PrefetchScalarGridSpec(
        num_scalar_prefetch=1,
        grid=(db, n_pages // g_pages),
        in_specs=[pl.BlockSpec(memory_space=pltpu.SMEM),
                  pl.BlockSpec((4, HEAD_QK), lambda b, j, pt: (0, 0)),
                  pl.BlockSpec((QROWS, PAGE_ROWS), lambda b, j, pt: (0, 0)),
                  pl.BlockSpec((1, HEAD_V), lambda b, j, pt: (0, 0)),
                  pl.BlockSpec((1, nrow, HEAD_V), lambda b, j, pt: (b, 0, 0)),
                  new_spec, new_spec]
                 + [page_spec(g) for g in range(g_pages)]
                 + [page_spec(g) for g in range(g_pages)],
        out_specs=new_spec,
        scratch_shapes=[pltpu.VMEM((2, nrow, PAGE_ROWS), F32),
                        pltpu.VMEM((nrow, 128), F32),
                        pltpu.VMEM((nrow, 128), F32),
                        pltpu.VMEM((nrow, HEAD_V), F32)],
    )
    return pl.pallas_call(
        _sattn_kernel,
        grid_spec=grid_spec,
        out_shape=jax.ShapeDtypeStruct((db, t_new, ATT_WIDTH), F32),
        compiler_params=_params("arbitrary", "arbitrary"),
        name="sample_attn",
    )(page_table, tbl, lamv, bkt, subg, q_rows, k_new, v_new,
      *([cache_k] * g_pages), *([cache_v] * g_pages))


def _prep_math(p, ps, mu, w0, a0, kkw, kaw, rkw, wwa, wg, ones_bd, out_refs):
    pm = p + mu * (ps - p)
    r = pm[:, 0:RWKV_WIDTH]
    kr = pm[:, RWKV_WIDTH:2 * RWKV_WIDTH]
    v = pm[:, 2 * RWKV_WIDTH:3 * RWKV_WIDTH]
    wa = pm[:, LORA_IN:LORA_IN + 128]
    lane = lax.broadcasted_iota(jnp.int32, wa.shape, 1)
    la = _dot(jnp.where(lane < 64, jnp.tanh(wa), wa).astype(BF16), wwa)
    z = -(w0 + la[:, :RWKV_WIDTH])
    softplus = jnp.maximum(z, 0.0) + jnp.log(1.0 + jnp.exp(-jnp.abs(z)))
    log_decay = -jnp.exp(-softplus - 0.5)
    asig = jax.nn.sigmoid(a0 + la[:, RWKV_WIDTH:])
    g = _dot(jax.nn.sigmoid(pm[:, GATE_IN:P_PAD]).astype(BF16), wg)
    kk = kr * kkw
    norm = jnp.sqrt(_dot((kk * kk).astype(BF16), ones_bd))
    kk = kk / jnp.maximum(norm, 1e-12)
    k2 = kr * (1.0 + (asig - 1.0) * kaw)
    bonus = _dot((r * k2 * rkw).astype(BF16), ones_bd) * v
    r_ref, lw_ref, k_ref, v_ref, a_ref, b_ref, g_ref, bonus_ref = out_refs
    r_ref[...] = r
    lw_ref[...] = log_decay
    k_ref[...] = k2
    v_ref[...] = v
    a_ref[...] = -kk
    b_ref[...] = kk * asig
    g_ref[...] = g
    bonus_ref[...] = bonus


def _prep_carry_kernel(p_ref, mu, w0, a0, kkw, kaw, rkw, wwa, wg, ones_bd, *rest):
    out_refs, carry = rest[:8], rest[8]
    i = pl.program_id(1)

    @pl.when(i == 0)
    def _():
        carry[...] = jnp.zeros(carry.shape, F32)

    p = p_ref[...]
    row = lax.broadcasted_iota(jnp.int32, p.shape, 0)
    ps = jnp.where(row == 0, carry[...], pltpu.roll(p, 1, 0))
    carry[...] = p[p.shape[0] - 1:, :]
    _prep_math(p, ps, mu[...], w0[...], a0[...], kkw[...], kaw[...], rkw[...], wwa[...], wg[...],
               ones_bd[...], out_refs)


def _prep_shift_kernel(p_ref, ps_ref, mu, w0, a0, kkw, kaw, rkw, wwa, wg, ones_bd, *out_refs):
    _prep_math(p_ref[...], ps_ref[...], mu[...], w0[...], a0[...], kkw[...], kaw[...], rkw[...], wwa[...],
               wg[...], ones_bd[...], out_refs)


def _rwkv_prep(p, p_shift, consts, n_seq, tm):
    m = p.shape[0]
    per_seq = m // n_seq // tm if p_shift is None else 0
    if p_shift is None:
        grid = (n_seq, per_seq)
        tile = lambda w: pl.BlockSpec((tm, w), lambda s, i: (s * per_seq + i, 0))
        const = lambda a: pl.BlockSpec(a.shape, lambda s, i: (0,) * a.ndim)
        kern, args, sem = _prep_carry_kernel, (p,), ("arbitrary", "arbitrary")
        scratch = [pltpu.VMEM((1, P_PAD), F32)]
    else:
        grid = (m // tm,)
        tile = lambda w: pl.BlockSpec((tm, w), lambda i: (i, 0))
        const = lambda a: pl.BlockSpec(a.shape, lambda i: (0,) * a.ndim)
        kern, args, sem = _prep_shift_kernel, (p, p_shift), ("arbitrary",)
        scratch = []
    return pl.pallas_call(
        kern,
        grid=grid,
        in_specs=[tile(P_PAD)] * len(args) + [const(a) for a in consts],
        out_specs=[tile(RWKV_WIDTH)] * 8,
        out_shape=[jax.ShapeDtypeStruct((m, RWKV_WIDTH), F32)] * 8,
        scratch_shapes=scratch,
        compiler_params=_params(*sem),
        name="rwkv_prep",
    )(*args, *consts)


def _wkv_kernel(r_ref, lw_ref, k_ref, v_ref, a_ref, b_ref, s0_ref, y_ref, sout_ref, s_scr):
    c = pl.program_id(1)
    nc = pl.num_programs(1)
    C = CHUNK
    pairs = range(RWKV_PAIRS)

    @pl.when(c == 0)
    def _():
        s_scr[...] = s0_ref[0]

    lw = lw_ref[0]
    trow = lax.broadcasted_iota(jnp.int32, (C, C), 0)
    tcol = lax.broadcasted_iota(jnp.int32, (C, C), 1)
    tri = jnp.where(tcol <= trow, 1.0, 0.0).astype(BF16)
    h1 = lw.astype(BF16)
    r1 = lw - h1.astype(F32)
    h2 = r1.astype(BF16)
    h3 = (r1 - h2.astype(F32)).astype(BF16)
    cs = _dot(tri, jnp.concatenate([h1, h2, h3], axis=1))
    cum = cs[:, :RWKV_WIDTH] + cs[:, RWKV_WIDTH:2 * RWKV_WIDTH] + cs[:, 2 * RWKV_WIDTH:]
    tot = cum[C - 1:C, :]
    e_inv = jnp.exp(-cum)
    e_tail = jnp.exp(tot - cum)
    at_all = a_ref[0] * jnp.exp(cum - lw)
    rt_all = r_ref[0] * jnp.exp(cum)
    bt_all = b_ref[0] * e_inv
    kt_all = k_ref[0] * e_inv
    bh_all = b_ref[0] * e_tail
    kh_all = k_ref[0] * e_tail
    w_tot = jnp.exp(tot)
    v_all = v_ref[0]

    row = lax.broadcasted_iota(jnp.int32, (2 * C, 2 * C), 0)
    col = lax.broadcasted_iota(jnp.int32, (2 * C, 2 * C), 1)
    same = (row // C) == (col // C)
    strict = same & (col < row)
    incl = same & (col <= row)
    eye = jnp.where(row == col, 1.0, 0.0)
    lane = lax.broadcasted_iota(jnp.int32, (C, 128), 1)
    first = lane < RWKV_HEAD
    bf = lambda x: x.astype(BF16)
    ls = [slice(pi * 128, (pi + 1) * 128) for pi in pairs]

    def stack(x):
        return jnp.concatenate([jnp.where(first, x, 0.0), jnp.where(first, 0.0, x)], axis=0)

    def dup(x):
        return jnp.concatenate([x, x], axis=0)

    at_s = [stack(at_all[:, s]) for s in ls]
    rt_s = [stack(rt_all[:, s]) for s in ls]
    v_s = [stack(v_all[:, s]) for s in ls]
    g = [_dot_nt(bf(jnp.concatenate([at_s[i], rt_s[i]], axis=0)),
                 bf(jnp.concatenate([dup(bt_all[:, ls[i]]), dup(kt_all[:, ls[i]])], axis=0))) for i in pairs]
    l_ab = [jnp.where(strict, g[i][0:2 * C, 0:2 * C], 0.0) for i in pairs]
    a_ak = [jnp.where(strict, g[i][0:2 * C, 2 * C:4 * C], 0.0) for i in pairs]
    a_rb = [jnp.where(incl, g[i][2 * C:4 * C, 0:2 * C], 0.0) for i in pairs]
    a_rk = [jnp.where(incl, g[i][2 * C:4 * C, 2 * C:4 * C], 0.0) for i in pairs]
    x = [eye + l for l in l_ab]
    pw = l_ab
    for _ in range(int(math.log2(C)) - 1):
        pw = [_dot(bf(m), bf(m)) for m in pw]
        x = [x[i] + _dot(bf(x[i]), bf(pw[i])) for i in pairs]
    av = [_dot(bf(jnp.concatenate([a_ak[i], a_rk[i]], axis=0)), bf(v_s[i])) for i in pairs]
    tx = [_dot(bf(x[i]), bf(jnp.concatenate([at_s[i], av[i][0:2 * C]], axis=1))) for i in pairs]
    s_old = [s_scr[i] for i in pairs]
    az = [_dot_nt(bf(jnp.concatenate([tx[i][:, 0:128], rt_s[i]], axis=0)), bf(s_old[i])) for i in pairs]
    u = [az[i][0:2 * C] + tx[i][:, 128:256] for i in pairs]
    y = [az[i][2 * C:4 * C] + _dot(bf(a_rb[i]), bf(u[i])) + av[i][2 * C:4 * C] for i in pairs]
    for i in pairs:
        y_ref[0, :, ls[i]] = y[i][0:C] + y[i][C:2 * C]
    upd = [_dot_tn(bf(jnp.concatenate([u[i], v_s[i]], axis=0)),
                   bf(jnp.concatenate([stack(bh_all[:, ls[i]]), stack(kh_all[:, ls[i]])], axis=0))) for i in pairs]
    for i in pairs:
        s_scr[i] = s_old[i] * w_tot[:, ls[i]] + upd[i]

    @pl.when(c == nc - 1)
    def _():
        sout_ref[0] = s_scr[...]


def _wkv(r, lw, k, v, a, b, s0_bd):
    n_seq, t, _ = r.shape
    seq_spec = pl.BlockSpec((1, CHUNK, RWKV_WIDTH), lambda s, c: (s, c, 0))
    st_spec = pl.BlockSpec((1, RWKV_PAIRS, 128, 128), lambda s, c: (s, 0, 0, 0))
    return pl.pallas_call(
        _wkv_kernel,
        grid=(n_seq, t // CHUNK),
        in_specs=[seq_spec] * 6 + [st_spec],
        out_specs=[seq_spec, st_spec],
        out_shape=[jax.ShapeDtypeStruct((n_seq, t, RWKV_WIDTH), F32),
                   jax.ShapeDtypeStruct((n_seq, RWKV_PAIRS, 128, 128), F32)],
        scratch_shapes=[pltpu.VMEM((RWKV_PAIRS, 128, 128), F32)],
        compiler_params=_params("arbitrary", "arbitrary"),
        name="wkv_scan",
    )(r, lw, k, v, a, b, s0_bd)


def _outproj_kernel(att_ref, y_ref, g_ref, bonus_ref, lng_ref, lnb_ref, ones_ref, wo_ref, x_ref, mod_ref,
                    gpost_ref, o_ref):
    y = y_ref[...]
    ones_bd = ones_ref[...]
    inv_n = 1.0 / RWKV_HEAD
    yc = y - _dot(y.astype(BF16), ones_bd) * inv_n
    var = _dot((yc * yc).astype(BF16), ones_bd) * inv_n
    yn = yc * lax.rsqrt(var + LNX_EPS)
    rw = (yn * lng_ref[...] + lnb_ref[...] + bonus_ref[...]) * g_ref[...]
    mix = (_dot(att_ref[...].astype(BF16), wo_ref[0:ATT_WIDTH, :])
           + _dot(rw.astype(BF16), wo_ref[ATT_WIDTH:, :]))
    o_ref[...] = x_ref[...] + mod_ref[0, 2] * _rms(mix, gpost_ref[...], RMS_EPS)


def _outproj(att, y, g, bonus, lng, lnb, ones_bd, w_out_b, x, mod4, g_post, tm, tiles_per_mod):
    m = x.shape[0]
    r = mod4.shape[2]
    tile = lambda w: pl.BlockSpec((tm, w), lambda i: (i, 0))
    const = lambda a: pl.BlockSpec(a.shape, lambda i: (0,) * a.ndim)
    return pl.pallas_call(
        _outproj_kernel,
        grid=(m // tm,),
        in_specs=[tile(ATT_WIDTH)] * 4 + [const(lng), const(lnb), const(ones_bd), const(w_out_b),
                                          tile(D_MODEL),
                                          pl.BlockSpec((1, 6, r, D_MODEL), lambda i: (i // tiles_per_mod, 0, 0, 0)),
                                          const(g_post)],
        out_specs=tile(D_MODEL),
        out_shape=jax.ShapeDtypeStruct((m, D_MODEL), F32),
        compiler_params=_params("arbitrary"),
        name="out_proj",
    )(att, y, g, bonus, lng, lnb, ones_bd, w_out_b, x, mod4, g_post)


FFN_TF = 1024


def _ffn_kernel(x_ref, mod_ref, gpre_ref, gpost_ref, wu_ref, wd_ref, o_ref, h_scr, acc_scr):
    f = pl.program_id(1)

    @pl.when(f == 0)
    def _():
        h = _rms(x_ref[...], gpre_ref[...], RMS_EPS) * (1.0 + mod_ref[0, 4]) + mod_ref[0, 3]
        h_scr[...] = h.astype(BF16)
        acc_scr[...] = jnp.zeros(acc_scr.shape, F32)

    u = jnp.maximum(_dot(h_scr[...], wu_ref[...]), 0.0)
    acc_scr[...] += _dot((u * u).astype(BF16), wd_ref[...])

    @pl.when(f == pl.num_programs(1) - 1)
    def _():
        o_ref[...] = x_ref[...] + mod_ref[0, 5] * _rms(acc_scr[...], gpost_ref[...], RMS_EPS)


def _ffn(x, mod4, g_pre, g_post, w_up_b, w_down_b, tm, tiles_per_mod):
    m = x.shape[0]
    r = mod4.shape[2]
    return pl.pallas_call(
        _ffn_kernel,
        grid=(m // tm, FFN_DIM // FFN_TF),
        in_specs=[pl.BlockSpec((tm, D_MODEL), lambda i, f: (i, 0)),
                  pl.BlockSpec((1, 6, r, D_MODEL), lambda i, f: (i // tiles_per_mod, 0, 0, 0)),
                  pl.BlockSpec((1, D_MODEL), lambda i, f: (0, 0)),
                  pl.BlockSpec((1, D_MODEL), lambda i, f: (0, 0)),
                  pl.BlockSpec((D_MODEL, FFN_TF), lambda i, f: (0, f)),
                  pl.BlockSpec((FFN_TF, D_MODEL), lambda i, f: (f, 0))],
        out_specs=pl.BlockSpec((tm, D_MODEL), lambda i, f: (i, 0)),
        out_shape=jax.ShapeDtypeStruct((m, D_MODEL), F32),
        scratch_shapes=[pltpu.VMEM((tm, D_MODEL), BF16), pltpu.VMEM((tm, D_MODEL), F32)],
        compiler_params=_params("arbitrary", "arbitrary"),
        name="ffn",
    )(x, mod4, g_pre, g_post, w_up_b, w_down_b)


def _state_to_blockdiag(s):
    n = s.shape[0]
    s = s.reshape(n, RWKV_PAIRS, 2, RWKV_HEAD, RWKV_HEAD)
    z = jnp.zeros((n, RWKV_PAIRS, RWKV_HEAD, RWKV_HEAD), s.dtype)
    top = jnp.concatenate([s[:, :, 0], z], axis=-1)
    bot = jnp.concatenate([z, s[:, :, 1]], axis=-1)
    return jnp.concatenate([top, bot], axis=-2)


def _blockdiag_to_state(sbd):
    n = sbd.shape[0]
    s = jnp.stack([sbd[:, :, :RWKV_HEAD, :RWKV_HEAD], sbd[:, :, RWKV_HEAD:, RWKV_HEAD:]], axis=2)
    return s.reshape(n, 2 * RWKV_PAIRS, RWKV_HEAD, RWKV_HEAD)


def _pad_cols(a, width):
    return jnp.pad(a, ((0, 0), (0, width - a.shape[1])))


def kernel(x_prompt, x_sample, cache_k, cache_v, state_wkv, state_shift, page_table, c_prompt, c_sample, bias_table, w_ada, b_ada, g_pre_mix, g_post_mix, g_pre_ffn, g_post_ffn, w_in, mu_shift, w0, w_lora_w, a0, w_lora_a, w_lora_g, k_k, k_a, r_k, lnx_g, lnx_b, lam_q1, lam_k1, lam_q2, lam_k2, subln_g, w_out, w_ffn_up, w_ffn_down):
    bsz, seq, d = x_prompt.shape
    dbs, tdec, _ = x_sample.shape
    n_p, n_s = bsz * seq, dbs * tdec

    w_in_b = jnp.concatenate([w_in[0][:, 3 * ATT_WIDTH:], jnp.zeros((d, P_PAD - RWKV_PROJ), F32),
                              w_in[0][:, :3 * ATT_WIDTH]], axis=1).astype(BF16)
    w_out_b = w_out[0].astype(BF16)
    w_up_b = w_ffn_up[0].astype(BF16)
    w_down_b = w_ffn_down[0].astype(BF16)
    wwa = jnp.zeros((128, 2 * RWKV_WIDTH), F32)
    wwa = wwa.at[:64, :RWKV_WIDTH].set(w_lora_w[0]).at[64:, RWKV_WIDTH:].set(w_lora_a[0]).astype(BF16)
    wg = jnp.pad(w_lora_g[0], ((0, P_PAD - GATE_IN - w_lora_g.shape[1]), (0, 0))).astype(BF16)
    head_id = jnp.arange(RWKV_WIDTH) // RWKV_HEAD
    ones_bd = (head_id[:, None] == head_id[None, :]).astype(BF16)
    prep_consts = (_pad_cols(mu_shift, P_PAD), w0, a0, k_k, k_a, r_k.reshape(1, RWKV_WIDTH), wwa, wg, ones_bd)
    lamv = jnp.concatenate([lam_q1, lam_k1, lam_q2, lam_k2], axis=0)

    n_c = bsz + dbs
    c_all = jnp.pad(jnp.concatenate([c_prompt, c_sample], axis=0), ((0, (-n_c) % 8), (0, 0)))
    mod = _ada(c_all, w_ada[0], b_ada)
    mod_p = mod[:bsz].reshape(bsz, 6, 1, d)
    mod_s = jnp.repeat(mod[bsz:n_c].reshape(dbs, 6, d), tdec, axis=0).transpose(1, 0, 2).reshape(1, 6, n_s, d)

    xp = x_prompt.reshape(n_p, d)
    xs = x_sample.reshape(n_s, d)
    tm_p = min(1024, seq)
    proj_p, qkvb_p = _inproj(xp, mod_p, g_pre_mix, w_in_b, tm_p, seq // tm_p)
    proj_s, qkvb_s = _inproj(xs, mod_s, g_pre_mix, w_in_b, n_s, 1)
    k_col, v_col = P_PAD + ATT_WIDTH, P_PAD + 2 * ATT_WIDTH
    k_p, v_p = proj_p[:, k_col:v_col], proj_p[:, v_col:]
    k_s, v_s, p_s = proj_s[:, k_col:v_col], proj_s[:, v_col:], proj_s[:, :P_PAD]

    att_p = _prompt_attn(qkvb_p.reshape(bsz, seq, 3 * ATT_WIDTH), bias_table, lamv, subln_g)
    q4 = qkvb_s[:, :ATT_WIDTH].reshape(dbs, tdec, ATT_HEADS, HEAD_V).transpose(0, 2, 1, 3)
    first = jnp.arange(HEAD_V) < HEAD_QK
    zq = jnp.zeros((dbs, ATT_HEADS, 8 - tdec, HEAD_V), BF16)
    q_rows = jnp.concatenate([jnp.where(first, q4, 0), zq, jnp.where(first, 0, q4), zq], axis=2)
    n_pool = cache_k.shape[1]
    att_s = _sample_attn(q_rows.reshape(dbs, ATT_HEADS * QROWS, HEAD_V),
                         k_s.reshape(dbs, tdec, ATT_WIDTH), v_s.reshape(dbs, tdec, ATT_WIDTH),
                         cache_k[0].reshape(n_pool, PAGE_ROWS, HEAD_V), cache_v[0].reshape(n_pool, PAGE_ROWS, HEAD_V),
                         page_table, bias_table, lamv, subln_g)

    pre_p = _rwkv_prep(proj_p, None, prep_consts, bsz, min(256, seq))
    p_s3 = p_s.reshape(dbs, tdec, P_PAD)
    shift_s = jnp.concatenate([_pad_cols(state_shift[0], P_PAD)[:, None, :], p_s3[:, :-1]], axis=1)
    pre_s = _rwkv_prep(p_s, shift_s.reshape(n_s, P_PAD), prep_consts, dbs, n_s)
    r_p, lw_p, kk_p, vv_p, a_p, b_p, g_p, bonus_p = pre_p
    r_s, lw_s, kk_s, vv_s, a_s, b_s, g_s, bonus_s = pre_s
    seq3 = lambda t: t.reshape(bsz, seq, RWKV_WIDTH)
    y_p, sbd_p = _wkv(seq3(r_p), seq3(lw_p), seq3(kk_p), seq3(vv_p), seq3(a_p), seq3(b_p),
                      jnp.zeros((bsz, RWKV_PAIRS, 128, 128), F32))
    dec3 = lambda t: jnp.pad(t.reshape(dbs, tdec, RWKV_WIDTH), ((0, 0), (0, CHUNK - tdec), (0, 0)))
    y_s, sbd_s = _wkv(dec3(r_s), dec3(lw_s), dec3(kk_s), dec3(vv_s), dec3(a_s), dec3(b_s),
                      _state_to_blockdiag(state_wkv[0]))
    y_s = y_s[:, :tdec].reshape(n_s, RWKV_WIDTH)

    tm_o = min(256, seq)
    x1_p = _outproj(att_p.reshape(n_p, ATT_WIDTH), y_p.reshape(n_p, RWKV_WIDTH), g_p, bonus_p, lnx_g, lnx_b,
                    ones_bd, w_out_b, xp, mod_p, g_post_mix, tm_o, seq // tm_o)
    x1_s = _outproj(att_s.reshape(n_s, ATT_WIDTH), y_s, g_s, bonus_s, lnx_g, lnx_b,
                    ones_bd, w_out_b, xs, mod_s, g_post_mix, n_s, 1)
    tm_f = min(512, seq)
    out_p = _ffn(x1_p, mod_p, g_pre_ffn, g_post_ffn, w_up_b, w_down_b, tm_f, seq // tm_f)
    out_s = _ffn(x1_s, mod_s, g_pre_ffn, g_post_ffn, w_up_b, w_down_b, n_s, 1)

    return (out_p.reshape(bsz, seq, d),
            out_s.reshape(dbs, tdec, d),
            k_p.reshape(1, bsz, seq, ATT_HEADS, HEAD_V),
            v_p.reshape(1, bsz, seq, ATT_HEADS, HEAD_V),
            _blockdiag_to_state(sbd_p)[None],
            proj_p.reshape(bsz, seq, PROJ_W)[None, :, -1, :RWKV_PROJ],
            k_s.reshape(1, dbs, tdec, ATT_HEADS, HEAD_V),
            v_s.reshape(1, dbs, tdec, ATT_HEADS, HEAD_V),
            _blockdiag_to_state(sbd_s)[None],
            p_s3[None, :, -1, :RWKV_PROJ])
```

```python
import functools
import math

import numpy as np
import jax
import jax.numpy as jnp
from jax import lax
from jax.experimental import pallas as pl
from jax.experimental.pallas import tpu as pltpu

F32 = jnp.float32
BF16 = jnp.bfloat16

D_MODEL = 2048
ATT_WIDTH = 1024
RWKV_WIDTH = 1024
ATT_HEADS = 8
HEAD_V = 128
HEAD_QK = 64
RWKV_HEAD = 64
RWKV_PAIRS = RWKV_WIDTH // 128
RWKV_PROJ = 3360
P_PAD = 3584
LORA_IN = 3072
GATE_IN = 3200
FFN_DIM = 8192
N_BUCKETS = 32
MAX_DISTANCE = 128
PAGE = 128
ATT_SCALE = HEAD_QK ** -0.5
RMS_EPS = 1e-6
SUBLN_EPS = 1e-5
LNX_EPS = 64e-5
NEG_INF = -1e30
LOG2E = math.log2(math.e)
LAMBDA_INIT = 0.8 - 0.6 * math.exp(-0.3 * 0)
CHUNK = 64
VMEM_LIMIT = 56 * 1024 * 1024


def _params(*sem):
    return pltpu.CompilerParams(dimension_semantics=sem, vmem_limit_bytes=VMEM_LIMIT)


def _dot(a, b):
    return jnp.dot(a, b, preferred_element_type=F32)


def _dot_nt(a, b):
    return lax.dot_general(a, b, (((1,), (1,)), ((), ())), preferred_element_type=F32)


def _dot_tn(a, b):
    return lax.dot_general(a, b, (((0,), (0,)), ((), ())), preferred_element_type=F32)


def _t5_bucket_np(dist):
    max_exact = N_BUCKETS // 2
    d = np.maximum(dist, 0)
    ratio = np.log(np.maximum(d, 1).astype(np.float32) / max_exact) / math.log(MAX_DISTANCE / max_exact)
    large = np.minimum(max_exact + (ratio * (N_BUCKETS - max_exact)).astype(np.int32), N_BUCKETS - 1)
    return np.where(d < max_exact, d, large).astype(np.int32)


def _table_lookup(bucket, tbl_ref, h):
    out = jnp.zeros(bucket.shape, F32)
    for b in range(N_BUCKETS):
        out = jnp.where(bucket == b, tbl_ref[b, h], out)
    return out


def _lam(lamv_ref):
    v = lamv_ref[...]
    s1 = jnp.sum(v[0:1] * v[1:2], axis=-1, keepdims=True)
    s2 = jnp.sum(v[2:3] * v[3:4], axis=-1, keepdims=True)
    return jnp.exp(s1) - jnp.exp(s2) + LAMBDA_INIT


def _rms(x, g, eps):
    return x * lax.rsqrt(jnp.mean(x * x, axis=-1, keepdims=True) + eps) * g


def _lanes(x, reps):
    return jnp.concatenate([x] * reps, axis=1)


def _ada_kernel(c_ref, w_ref, b_ref, o_ref):
    c = c_ref[...]
    x = (c * jax.nn.sigmoid(c)).astype(BF16)
    o_ref[...] = _dot(x, w_ref[...].astype(BF16)) + b_ref[...]


def _ada(c_all, w_ada, b_ada):
    rows, n = c_all.shape[0], w_ada.shape[1]
    tn = 1536
    return pl.pallas_call(
        _ada_kernel,
        grid=(n // tn,),
        in_specs=[pl.BlockSpec((rows, D_MODEL), lambda j: (0, 0)),
                  pl.BlockSpec((D_MODEL, tn), lambda j: (0, j)),
                  pl.BlockSpec((1, tn), lambda j: (0, j))],
        out_specs=pl.BlockSpec((rows, tn), lambda j: (0, j)),
        out_shape=jax.ShapeDtypeStruct((rows, n), F32),
        compiler_params=_params("arbitrary"),
        name="ada_mod",
    )(c_all, w_ada, b_ada)


IN_TN = 512


PROJ_W = P_PAD + 3 * ATT_WIDTH
NPJ = P_PAD // IN_TN


def _inproj_kernel(x_ref, mod_ref, g_ref, w_ref, proj_ref, qkvb_ref, h_scr):
    j = pl.program_id(1)

    @pl.when(j == 0)
    def _():
        h = _rms(x_ref[...], g_ref[...], RMS_EPS) * (1.0 + mod_ref[0, 1]) + mod_ref[0, 0]
        h_scr[...] = h.astype(BF16)

    proj_ref[...] = _dot(h_scr[...], w_ref[...])

    @pl.when(j >= NPJ)
    def _():
        scale = jnp.where(j < NPJ + ATT_WIDTH // IN_TN, ATT_SCALE * LOG2E, 1.0)
        qkvb_ref[...] = (proj_ref[...] * scale).astype(BF16)


def _inproj(x, mod4, g_pre, w_in_b, tm, tiles_per_mod):
    m = x.shape[0]
    r = mod4.shape[2]
    nj = PROJ_W // IN_TN
    return pl.pallas_call(
        _inproj_kernel,
        grid=(m // tm, nj),
        in_specs=[pl.BlockSpec((tm, D_MODEL), lambda i, j: (i, 0)),
                  pl.BlockSpec((1, 6, r, D_MODEL), lambda i, j: (i // tiles_per_mod, 0, 0, 0)),
                  pl.BlockSpec((1, D_MODEL), lambda i, j: (0, 0)),
                  pl.BlockSpec((D_MODEL, IN_TN), lambda i, j: (0, j))],
        out_specs=[pl.BlockSpec((tm, IN_TN), lambda i, j: (i, j)),
                   pl.BlockSpec((tm, IN_TN), lambda i, j: (i, jnp.maximum(j - NPJ, 0)))],
        out_shape=[jax.ShapeDtypeStruct((m, PROJ_W), F32),
                   jax.ShapeDtypeStruct((m, 3 * ATT_WIDTH), BF16)],
        scratch_shapes=[pltpu.VMEM((tm, D_MODEL), BF16)],
        compiler_params=_params("arbitrary", "arbitrary"),
        name="in_proj",
    )(x, mod4, g_pre, w_in_b)


ATT_T = 512
ATT_SUB = 128


ATT_HPS = 4


def _pattn_kernel(qi_ref, ki_ref, tbl_ref, lamv_ref, bkt_ref, subg_ref, q_ref, k_ref, v_ref, o_ref,
                  bias_scr, q2_scr, m_scr, l_scr, acc_scr):
    hg = pl.program_id(1)
    step = pl.program_id(2)
    qi = qi_ref[step]
    ki = ki_ref[step]
    nsub = ATT_T // ATT_SUB
    heads = range(ATT_HPS)
    lanes = [slice(u * HEAD_V, (u + 1) * HEAD_V) for u in heads]

    @pl.when(step == 0)
    def _():
        row = lax.broadcasted_iota(jnp.int32, (ATT_SUB, ATT_SUB), 0)
        col = lax.broadcasted_iota(jnp.int32, (ATT_SUB, ATT_SUB), 1)
        for u in heads:
            h = hg * ATT_HPS + u
            t0 = jnp.where(col > row, NEG_INF, _table_lookup(bkt_ref[0], tbl_ref, h) * LOG2E)
            t1 = _table_lookup(bkt_ref[1], tbl_ref, h) * LOG2E
            far = jnp.full((ATT_SUB, ATT_SUB), tbl_ref[N_BUCKETS - 1, h] * LOG2E, F32)
            masked = jnp.full((ATT_SUB, ATT_SUB), NEG_INF, F32)
            for rb in range(nsub):
                for cb in range(nsub):
                    d = rb - cb
                    diag = t0 if d == 0 else t1 if d == 1 else far if d >= 2 else masked
                    off = t1 if (rb == 0 and cb == nsub - 1) else far
                    rs, cs = slice(rb * ATT_SUB, (rb + 1) * ATT_SUB), slice(cb * ATT_SUB, (cb + 1) * ATT_SUB)
                    bias_scr[u, 0, rs, cs] = diag
                    bias_scr[u, 1, rs, cs] = off
                    bias_scr[u, 2, rs, cs] = far

    @pl.when(ki == 0)
    def _():
        lane = lax.broadcasted_iota(jnp.int32, (ATT_T, HEAD_V), 1)
        for u in heads:
            q = q_ref[0, :, lanes[u]]
            zero = jnp.zeros_like(q)
            q2_scr[u, 0:ATT_T, :] = jnp.where(lane < HEAD_QK, q, zero)
            q2_scr[u, ATT_T:, :] = jnp.where(lane >= HEAD_QK, q, zero)
        m_scr[...] = jnp.full(m_scr.shape, NEG_INF, F32)
        l_scr[...] = jnp.zeros(l_scr.shape, F32)
        acc_scr[...] = jnp.zeros(acc_scr.shape, F32)

    which = jnp.minimum(qi - ki, 2)
    s = []
    for u in heads:
        bias = bias_scr[u, which]
        s.append(_dot_nt(q2_scr[u], k_ref[0, :, lanes[u]]) + jnp.concatenate([bias, bias], axis=0))
    m_prev = [m_scr[u] for u in heads]
    m_new = [jnp.maximum(m_prev[u], jnp.max(s[u], axis=-1, keepdims=True)) for u in heads]
    alpha = [jnp.exp2(m_prev[u] - m_new[u]) for u in heads]
    p = [jnp.exp2(s[u] - _lanes(m_new[u], ATT_T // 128)) for u in heads]
    for u in heads:
        l_scr[u] = alpha[u] * l_scr[u] + jnp.sum(p[u], axis=-1, keepdims=True)
        acc_scr[u] = alpha[u] * acc_scr[u] + _dot(p[u].astype(BF16), v_ref[0, :, lanes[u]])
        m_scr[u] = m_new[u]

    @pl.when(ki == qi)
    def _():
        lam = _lam(lamv_ref)
        for u in heads:
            on = acc_scr[u] / l_scr[u]
            o = on[0:ATT_T] - lam * on[ATT_T:]
            o = _rms(o, subg_ref[...], SUBLN_EPS) * (1.0 - LAMBDA_INIT)
            o_ref[0, :, lanes[u]] = o.astype(BF16)


def _prompt_attn(qkv, tbl, lamv, subg):
    b, s, _ = qkv.shape
    nq = s // ATT_T
    r = np.arange(ATT_SUB)
    dist = r[:, None] - r[None, :]
    bkt = jnp.asarray(np.stack([_t5_bucket_np(dist), _t5_bucket_np(dist + ATT_SUB)]))
    pairs = [(qi, ki) for qi in range(nq) for ki in range(qi + 1)]
    qi_of = jnp.asarray(np.array([p[0] for p in pairs], np.int32))
    ki_of = jnp.asarray(np.array([p[1] for p in pairs], np.int32))
    groups = ATT_HEADS // ATT_HPS
    width = ATT_HPS * HEAD_V
    q_spec = pl.BlockSpec((1, ATT_T, width), lambda bi, h, st, qi, ki: (bi, qi[st], h))
    k_spec = pl.BlockSpec((1, ATT_T, width), lambda bi, h, st, qi, ki: (bi, ki[st], groups + h))
    v_spec = pl.BlockSpec((1, ATT_T, width), lambda bi, h, st, qi, ki: (bi, ki[st], 2 * groups + h))
    const = lambda shape: pl.BlockSpec(shape, lambda bi, h, st, qi, ki: (0,) * len(shape))
    grid_spec = pltpu.PrefetchScalarGridSpec(
        num_scalar_prefetch=2,
        grid=(b, groups, len(pairs)),
        in_specs=[pl.BlockSpec(memory_space=pltpu.SMEM), const((4, HEAD_QK)), const((2, ATT_SUB, ATT_SUB)),
                  const((1, HEAD_V)), q_spec, k_spec, v_spec],
        out_specs=q_spec,
        scratch_shapes=[pltpu.VMEM((ATT_HPS, 3, ATT_T, ATT_T), F32),
                        pltpu.VMEM((ATT_HPS, 2 * ATT_T, HEAD_V), BF16),
                        pltpu.VMEM((ATT_HPS, 2 * ATT_T, 128), F32),
                        pltpu.VMEM((ATT_HPS, 2 * ATT_T, 128), F32),
                        pltpu.VMEM((ATT_HPS, 2 * ATT_T, HEAD_V), F32)],
    )
    return pl.pallas_call(
        _pattn_kernel,
        grid_spec=grid_spec,
        out_shape=jax.ShapeDtypeStruct((b, s, ATT_WIDTH), BF16),
        compiler_params=_params("arbitrary", "arbitrary", "arbitrary"),
        name="prompt_attn",
    )(qi_of, ki_of, tbl, lamv, bkt, subg, qkv, qkv, qkv)


PAGES_PER_STEP = 16
QROWS = 16
PAGE_ROWS = PAGE * ATT_HEADS


def _sattn_kernel(pt_ref, tbl_ref, lamv_ref, bkt_ref, subg_ref, q_ref, kn_ref, vn_ref, *rest):
    g_pages = PAGES_PER_STEP
    k_refs = rest[:g_pages]
    v_refs = rest[g_pages:2 * g_pages]
    o_ref, bias_scr, m_scr, l_scr, acc_scr = rest[2 * g_pages:]
    del pt_ref
    b = pl.program_id(0)
    j = pl.program_id(1)
    nj = pl.num_programs(1)
    n_new = kn_ref.shape[1]
    nrow = ATT_HEADS * QROWS

    @pl.when((b == 0) & (j == 0))
    def _():
        key_head = lax.broadcasted_iota(jnp.int32, (QROWS, PAGE_ROWS), 1) % ATT_HEADS
        for h in range(ATT_HEADS):
            rows = slice(h * QROWS, (h + 1) * QROWS)
            bias_scr[0, rows, :] = jnp.where(key_head == h, tbl_ref[N_BUCKETS - 1, h] * LOG2E, NEG_INF)
            bias_scr[1, rows, :] = jnp.where(key_head == h, _table_lookup(bkt_ref[...], tbl_ref, h) * LOG2E, NEG_INF)

    @pl.when(j == 0)
    def _():
        m_scr[...] = jnp.full(m_scr.shape, NEG_INF, F32)
        l_scr[...] = jnp.zeros(l_scr.shape, F32)
        acc_scr[...] = jnp.zeros(acc_scr.shape, F32)

    q = q_ref[0]
    s_pages = []
    for g in range(g_pages):
        bias = bias_scr[(j == nj - 1).astype(jnp.int32)] if g == g_pages - 1 else bias_scr[0]
        s_pages.append(_dot_nt(q, k_refs[g][0].astype(BF16)) + bias)
    m_run = m_scr[...]
    m_new = m_run
    for s in s_pages:
        m_new = jnp.maximum(m_new, jnp.max(s, axis=-1, keepdims=True))
    alpha = jnp.exp2(m_run - m_new)
    m_lanes = _lanes(m_new, PAGE_ROWS // 128)
    l_run = alpha * l_scr[...]
    acc = alpha * acc_scr[...]
    for g in range(g_pages):
        p = jnp.exp2(s_pages[g] - m_lanes)
        l_run = l_run + jnp.sum(p, axis=-1, keepdims=True)
        acc = acc + _dot(p.astype(BF16), v_refs[g][0].astype(BF16))
    m_run = m_new
    m_scr[...] = m_run
    l_scr[...] = l_run
    acc_scr[...] = acc

    @pl.when(j == nj - 1)
    def _():
        lam = _lam(lamv_ref)
        qf = q.astype(F32)
        t_row = lax.broadcasted_iota(jnp.int32, (nrow, 1), 0) % 8
        head_of_row = lax.broadcasted_iota(jnp.int32, (nrow, 1), 0) // QROWS
        m_fin, l_fin, acc_fin = m_run, l_run, acc
        tbl_rows = []
        for dd in range(n_new):
            tbl_row = jnp.zeros((nrow, 1), F32)
            for h in range(ATT_HEADS):
                tbl_row = jnp.where(head_of_row == h, tbl_ref[dd, h] * LOG2E, tbl_row)
            tbl_rows.append(tbl_row)
        for tn in range(n_new):
            d = t_row - tn
            bias = jnp.full((nrow, 1), NEG_INF, F32)
            for dd in range(n_new):
                bias = jnp.where(d == dd, tbl_rows[dd], bias)
            k_rows = jnp.concatenate(
                [jnp.broadcast_to(kn_ref[0, tn:tn + 1, h * HEAD_V:(h + 1) * HEAD_V], (QROWS, HEAD_V))
                 for h in range(ATT_HEADS)], axis=0)
            v_rows = jnp.concatenate(
                [jnp.broadcast_to(vn_ref[0, tn:tn + 1, h * HEAD_V:(h + 1) * HEAD_V], (QROWS, HEAD_V))
                 for h in range(ATT_HEADS)], axis=0)
            s = jnp.sum(qf * k_rows, axis=-1, keepdims=True) + bias
            m_new = jnp.maximum(m_fin, s)
            alpha = jnp.exp2(m_fin - m_new)
            p = jnp.exp2(s - m_new)
            l_fin = alpha * l_fin + p
            acc_fin = alpha * acc_fin + p * v_rows
            m_fin = m_new
        on = acc_fin / l_fin
        for h in range(ATT_HEADS):
            o = on[h * QROWS:h * QROWS + 8] - lam * on[h * QROWS + 8:(h + 1) * QROWS]
            o = _rms(o, subg_ref[...], SUBLN_EPS) * (1.0 - LAMBDA_INIT)
            o_ref[0, :, h * HEAD_V:(h + 1) * HEAD_V] = o[0:n_new]


def _sample_attn(q_rows, k_new, v_new, cache_k, cache_v, page_table, tbl, lamv, subg):
    db, t_new = k_new.shape[0], k_new.shape[1]
    n_pages = page_table.shape[1]
    g_pages = PAGES_PER_STEP
    nrow = ATT_HEADS * QROWS
    t_row = (np.arange(QROWS) % 8)[:, None]
    tok = (np.arange(PAGE_ROWS) // ATT_HEADS)[None, :]
    bkt = jnp.asarray(_t5_bucket_np(PAGE + t_row - tok))

    def page_spec(g):
        return pl.BlockSpec((1, PAGE_ROWS, HEAD_V), lambda b, j, pt: (pt[b, j * g_pages + g], 0, 0))

    new_spec = pl.BlockSpec((1, t_new, ATT_WIDTH), lambda b, j, pt: (b, 0, 0))
    grid_spec = pltpu.PrefetchScalarGridSpec(
        num_scalar_prefetch=1,
        grid=(db, n_pages // g_pages),
        in_specs=[pl.BlockSpec(memory_space=pltpu.SMEM),
                  pl.BlockSpec((4, HEAD_QK), lambda b, j, pt: (0, 0)),
                  pl.BlockSpec((QROWS, PAGE_ROWS), lambda b, j, pt: (0, 0)),
                  pl.BlockSpec((1, HEAD_V), lambda b, j, pt: (0, 0)),
                  pl.BlockSpec((1, nrow, HEAD_V), lambda b, j, pt: (b, 0, 0)),
                  new_spec, new_spec]
                 + [page_spec(g) for g in range(g_pages)]
                 + [page_spec(g) for g in range(g_pages)],
        out_specs=new_spec,
        scratch_shapes=[pltpu.VMEM((2, nrow, PAGE_ROWS), F32),
                        pltpu.VMEM((nrow, 128), F32),
                        pltpu.VMEM((nrow, 128), F32),
                        pltpu.VMEM((nrow, HEAD_V), F32)],
    )
    return pl.pallas_call(
        _sattn_kernel,
        grid_spec=grid_spec,
        out_shape=jax.ShapeDtypeStruct((db, t_new, ATT_WIDTH), F32),
        compiler_params=_params("arbitrary", "arbitrary"),
        name="sample_attn",
    )(page_table, tbl, lamv, bkt, subg, q_rows, k_new, v_new,
      *([cache_k] * g_pages), *([cache_v] * g_pages))


def _prep_math(p, ps, mu, w0, a0, kkw, kaw, rkw, wwa, wg, ones_bd, out_refs):
    pm = p + mu * (ps - p)
    r = pm[:, 0:RWKV_WIDTH]
    kr = pm[:, RWKV_WIDTH:2 * RWKV_WIDTH]
    v = pm[:, 2 * RWKV_WIDTH:3 * RWKV_WIDTH]
    wa = pm[:, LORA_IN:LORA_IN + 128]
    lane = lax.broadcasted_iota(jnp.int32, wa.shape, 1)
    la = _dot(jnp.where(lane < 64, jnp.tanh(wa), wa).astype(BF16), wwa)
    z = -(w0 + la[:, :RWKV_WIDTH])
    softplus = jnp.maximum(z, 0.0) + jnp.log(1.0 + jnp.exp(-jnp.abs(z)))
    log_decay = -jnp.exp(-softplus - 0.5)
    asig = jax.nn.sigmoid(a0 + la[:, RWKV_WIDTH:])
    g = _dot(jax.nn.sigmoid(pm[:, GATE_IN:P_PAD]).astype(BF16), wg)
    kk = kr * kkw
    norm = jnp.sqrt(_dot((kk * kk).astype(BF16), ones_bd))
    kk = kk / jnp.maximum(norm, 1e-12)
    k2 = kr * (1.0 + (asig - 1.0) * kaw)
    bonus = _dot((r * k2 * rkw).astype(BF16), ones_bd) * v
    r_ref, lw_ref, k_ref, v_ref, a_ref, b_ref, g_ref, bonus_ref = out_refs
    r_ref[...] = r
    lw_ref[...] = log_decay
    k_ref[...] = k2
    v_ref[...] = v
    a_ref[...] = -kk
    b_ref[...] = kk * asig
    g_ref[...] = g
    bonus_ref[...] = bonus


def _prep_carry_kernel(p_ref, mu, w0, a0, kkw, kaw, rkw, wwa, wg, ones_bd, *rest):
    out_refs, carry = rest[:8], rest[8]
    i = pl.program_id(1)

    @pl.when(i == 0)
    def _():
        carry[...] = jnp.zeros(carry.shape, F32)

    p = p_ref[...]
    row = lax.broadcasted_iota(jnp.int32, p.shape, 0)
    ps = jnp.where(row == 0, carry[...], pltpu.roll(p, 1, 0))
    carry[...] = p[p.shape[0] - 1:, :]
    _prep_math(p, ps, mu[...], w0[...], a0[...], kkw[...], kaw[...], rkw[...], wwa[...], wg[...],
               ones_bd[...], out_refs)


def _prep_shift_kernel(p_ref, ps_ref, mu, w0, a0, kkw, kaw, rkw, wwa, wg, ones_bd, *out_refs):
    _prep_math(p_ref[...], ps_ref[...], mu[...], w0[...], a0[...], kkw[...], kaw[...], rkw[...], wwa[...],
               wg[...], ones_bd[...], out_refs)


def _rwkv_prep(p, p_shift, consts, n_seq, tm):
    m = p.shape[0]
    per_seq = m // n_seq // tm if p_shift is None else 0
    if p_shift is None:
        grid = (n_seq, per_seq)
        tile = lambda w: pl.BlockSpec((tm, w), lambda s, i: (s * per_seq + i, 0))
        const = lambda a: pl.BlockSpec(a.shape, lambda s, i: (0,) * a.ndim)
        kern, args, sem = _prep_carry_kernel, (p,), ("arbitrary", "arbitrary")
        scratch = [pltpu.VMEM((1, P_PAD), F32)]
    else:
        grid = (m // tm,)
        tile = lambda w: pl.BlockSpec((tm, w), lambda i: (i, 0))
        const = lambda a: pl.BlockSpec(a.shape, lambda i: (0,) * a.ndim)
        kern, args, sem = _prep_shift_kernel, (p, p_shift), ("arbitrary",)
        scratch = []
    return pl.pallas_call(
        kern,
        grid=grid,
        in_specs=[tile(P_PAD)] * len(args) + [const(a) for a in consts],
        out_specs=[tile(RWKV_WIDTH)] * 8,
        out_shape=[jax.ShapeDtypeStruct((m, RWKV_WIDTH), F32)] * 8,
        scratch_shapes=scratch,
        compiler_params=_params(*sem),
        name="rwkv_prep",
    )(*args, *consts)


def _wkv_kernel(r_ref, lw_ref, k_ref, v_ref, a_ref, b_ref, s0_ref, y_ref, sout_ref, s_scr):
    c = pl.program_id(1)
    nc = pl.num_programs(1)
    C = CHUNK
    pairs = range(RWKV_PAIRS)

    @pl.when(c == 0)
    def _():
        s_scr[...] = s0_ref[0]

    t_in = lw_ref.shape[1]

    def load(ref):
        x = ref[0]
        return x if t_in == C else jnp.concatenate([x, jnp.zeros((C - t_in, RWKV_WIDTH), F32)], axis=0)

    r_in, lw, k_in, v_all, a_in, b_in = (load(ref) for ref in (r_ref, lw_ref, k_ref, v_ref, a_ref, b_ref))
    trow = lax.broadcasted_iota(jnp.int32, (C, C), 0)
    tcol = lax.broadcasted_iota(jnp.int32, (C, C), 1)
    tri = jnp.where(tcol <= trow, 1.0, 0.0).astype(BF16)
    h1 = lw.astype(BF16)
    r1 = lw - h1.astype(F32)
    h2 = r1.astype(BF16)
    h3 = (r1 - h2.astype(F32)).astype(BF16)
    cs = _dot(tri, jnp.concatenate([h1, h2, h3], axis=1))
    cum = cs[:, :RWKV_WIDTH] + cs[:, RWKV_WIDTH:2 * RWKV_WIDTH] + cs[:, 2 * RWKV_WIDTH:]
    tot = cum[C - 1:C, :]
    e_inv = jnp.exp(-cum)
    e_tail = jnp.exp(tot - cum)
    at_all = a_in * jnp.exp(cum - lw)
    rt_all = r_in * jnp.exp(cum)
    bt_all = b_in * e_inv
    kt_all = k_in * e_inv
    bh_all = b_in * e_tail
    kh_all = k_in * e_tail
    w_tot = jnp.exp(tot)

    row = lax.broadcasted_iota(jnp.int32, (2 * C, 2 * C), 0)
    col = lax.broadcasted_iota(jnp.int32, (2 * C, 2 * C), 1)
    same = (row // C) == (col // C)
    strict = same & (col < row)
    incl = same & (col <= row)
    eye = jnp.where(row == col, 1.0, 0.0)
    lane = lax.broadcasted_iota(jnp.int32, (C, 128), 1)
    first = lane < RWKV_HEAD
    bf = lambda x: x.astype(BF16)
    ls = [slice(pi * 128, (pi + 1) * 128) for pi in pairs]

    def stack(x):
        return jnp.concatenate([jnp.where(first, x, 0.0), jnp.where(first, 0.0, x)], axis=0)

    def dup(x):
        return jnp.concatenate([x, x], axis=0)

    at_s = [stack(at_all[:, s]) for s in ls]
    rt_s = [stack(rt_all[:, s]) for s in ls]
    v_s = [stack(v_all[:, s]) for s in ls]
    g = [_dot_nt(bf(jnp.concatenate([at_s[i], rt_s[i]], axis=0)),
                 bf(jnp.concatenate([dup(bt_all[:, ls[i]]), dup(kt_all[:, ls[i]])], axis=0))) for i in pairs]
    l_ab = [jnp.where(strict, g[i][0:2 * C, 0:2 * C], 0.0) for i in pairs]
    a_ak = [jnp.where(strict, g[i][0:2 * C, 2 * C:4 * C], 0.0) for i in pairs]
    a_rb = [jnp.where(incl, g[i][2 * C:4 * C, 0:2 * C], 0.0) for i in pairs]
    a_rk = [jnp.where(incl, g[i][2 * C:4 * C, 2 * C:4 * C], 0.0) for i in pairs]
    x = [eye + l for l in l_ab]
    pw = l_ab
    for _ in range(int(math.log2(C)) - 1):
        pw = [_dot(bf(m), bf(m)) for m in pw]
        x = [x[i] + _dot(bf(x[i]), bf(pw[i])) for i in pairs]
    av = [_dot(bf(jnp.concatenate([a_ak[i], a_rk[i]], axis=0)), bf(v_s[i])) for i in pairs]
    tx = [_dot(bf(x[i]), bf(jnp.concatenate([at_s[i], av[i][0:2 * C]], axis=1))) for i in pairs]
    s_old = [s_scr[i] for i in pairs]
    az = [_dot_nt(bf(jnp.concatenate([tx[i][:, 0:128], rt_s[i]], axis=0)), bf(s_old[i])) for i in pairs]
    u = [az[i][0:2 * C] + tx[i][:, 128:256] for i in pairs]
    y = [az[i][2 * C:4 * C] + _dot(bf(a_rb[i]), bf(u[i])) + av[i][2 * C:4 * C] for i in pairs]
    for i in pairs:
        y_ref[0, :, ls[i]] = (y[i][0:C] + y[i][C:2 * C])[0:t_in]
    upd =[_dot_tn(bf(jnp.concatenate([u[i], v_s[i]], axis=0)),
                   bf(jnp.concatenate([stack(bh_all[:, ls[i]]), stack(kh_all[:, ls[i]])], axis=0))) for i in pairs]
    for i in pairs:
        s_scr[i] = s_old[i] * w_tot[:, ls[i]] + upd[i]

    @pl.when(c == nc - 1)
    def _():
        sout_ref[0] = s_scr[...]


def _wkv(r, lw, k, v, a, b, s0_bd):
    n_seq, t, _ = r.shape
    t_blk = min(t, CHUNK)
    seq_spec = pl.BlockSpec((1, t_blk, RWKV_WIDTH), lambda s, c: (s, c, 0))
    st_spec = pl.BlockSpec((1, RWKV_PAIRS, 128, 128), lambda s, c: (s, 0, 0, 0))
    return pl.pallas_call(
        _wkv_kernel,
        grid=(n_seq, t // t_blk),
        in_specs=[seq_spec] * 6 + [st_spec],
        out_specs=[seq_spec, st_spec],
        out_shape=[jax.ShapeDtypeStruct((n_seq, t, RWKV_WIDTH), F32),
                   jax.ShapeDtypeStruct((n_seq, RWKV_PAIRS, 128, 128), F32)],
        scratch_shapes=[pltpu.VMEM((RWKV_PAIRS, 128, 128), F32)],
        compiler_params=_params("arbitrary", "arbitrary"),
        name="wkv_scan",
    )(r, lw, k, v, a, b, s0_bd)


def _outproj_kernel(att_ref, y_ref, g_ref, bonus_ref, lng_ref, lnb_ref, ones_ref, wo_ref, x_ref, mod_ref,
                    gpost_ref, o_ref):
    y = y_ref[...]
    ones_bd = ones_ref[...]
    inv_n = 1.0 / RWKV_HEAD
    yc = y - _dot(y.astype(BF16), ones_bd) * inv_n
    var = _dot((yc * yc).astype(BF16), ones_bd) * inv_n
    yn = yc * lax.rsqrt(var + LNX_EPS)
    rw = (yn * lng_ref[...] + lnb_ref[...] + bonus_ref[...]) * g_ref[...]
    mix = (_dot(att_ref[...].astype(BF16), wo_ref[0:ATT_WIDTH, :])
           + _dot(rw.astype(BF16), wo_ref[ATT_WIDTH:, :]))
    o_ref[...] = x_ref[...] + mod_ref[0, 2] * _rms(mix, gpost_ref[...], RMS_EPS)


def _outproj(att, y, g, bonus, lng, lnb, ones_bd, w_out_b, x, mod4, g_post, tm, tiles_per_mod):
    m = x.shape[0]
    r = mod4.shape[2]
    tile = lambda w: pl.BlockSpec((tm, w), lambda i: (i, 0))
    const = lambda a: pl.BlockSpec(a.shape, lambda i: (0,) * a.ndim)
    return pl.pallas_call(
        _outproj_kernel,
        grid=(m // tm,),
        in_specs=[tile(ATT_WIDTH)] * 4 + [const(lng), const(lnb), const(ones_bd), const(w_out_b),
                                          tile(D_MODEL),
                                          pl.BlockSpec((1, 6, r, D_MODEL), lambda i: (i // tiles_per_mod, 0, 0, 0)),
                                          const(g_post)],
        out_specs=tile(D_MODEL),
        out_shape=jax.ShapeDtypeStruct((m, D_MODEL), F32),
        compiler_params=_params("arbitrary"),
        name="out_proj",
    )(att, y, g, bonus, lng, lnb, ones_bd, w_out_b, x, mod4, g_post)


FFN_TF = 1024


def _ffn_kernel(x_ref, mod_ref, gpre_ref, gpost_ref, wu_ref, wd_ref, o_ref, h_scr, acc_scr):
    f = pl.program_id(1)

    @pl.when(f == 0)
    def _():
        h = _rms(x_ref[...], gpre_ref[...], RMS_EPS) * (1.0 + mod_ref[0, 4]) + mod_ref[0, 3]
        h_scr[...] = h.astype(BF16)
        acc_scr[...] = jnp.zeros(acc_scr.shape, F32)

    u = jnp.maximum(_dot(h_scr[...], wu_ref[...]), 0.0)
    acc_scr[...] += _dot((u * u).astype(BF16), wd_ref[...])

    @pl.when(f == pl.num_programs(1) - 1)
    def _():
        o_ref[...] = x_ref[...] + mod_ref[0, 5] * _rms(acc_scr[...], gpost_ref[...], RMS_EPS)


def _ffn(x, mod4, g_pre, g_post, w_up_b, w_down_b, tm, tiles_per_mod):
    m = x.shape[0]
    r = mod4.shape[2]
    return pl.pallas_call(
        _ffn_kernel,
        grid=(m // tm, FFN_DIM // FFN_TF),
        in_specs=[pl.BlockSpec((tm, D_MODEL), lambda i, f: (i, 0)),
                  pl.BlockSpec((1, 6, r, D_MODEL), lambda i, f: (i // tiles_per_mod, 0, 0, 0)),
                  pl.BlockSpec((1, D_MODEL), lambda i, f: (0, 0)),
                  pl.BlockSpec((1, D_MODEL), lambda i, f: (0, 0)),
                  pl.BlockSpec((D_MODEL, FFN_TF), lambda i, f: (0, f)),
                  pl.BlockSpec((FFN_TF, D_MODEL), lambda i, f: (f, 0))],
        out_specs=pl.BlockSpec((tm, D_MODEL), lambda i, f: (i, 0)),
        out_shape=jax.ShapeDtypeStruct((m, D_MODEL), F32),
        scratch_shapes=[pltpu.VMEM((tm, D_MODEL), BF16), pltpu.VMEM((tm, D_MODEL), F32)],
        compiler_params=_params("arbitrary", "arbitrary"),
        name="ffn",
    )(x, mod4, g_pre, g_post, w_up_b, w_down_b)


def _state_to_blockdiag(s):
    n = s.shape[0]
    s = s.reshape(n, RWKV_PAIRS, 2, RWKV_HEAD, RWKV_HEAD)
    z = jnp.zeros((n, RWKV_PAIRS, RWKV_HEAD, RWKV_HEAD), s.dtype)
    top = jnp.concatenate([s[:, :, 0], z], axis=-1)
    bot = jnp.concatenate([z, s[:, :, 1]], axis=-1)
    return jnp.concatenate([top, bot], axis=-2)


def _blockdiag_to_state(sbd):
    n = sbd.shape[0]
    s = jnp.stack([sbd[:, :, :RWKV_HEAD, :RWKV_HEAD], sbd[:, :, RWKV_HEAD:, RWKV_HEAD:]], axis=2)
    return s.reshape(n, 2 * RWKV_PAIRS, RWKV_HEAD, RWKV_HEAD)


def _pad_cols(a, width):
    return jnp.pad(a, ((0, 0), (0, width - a.shape[1])))


def kernel(x_prompt, x_sample, cache_k, cache_v, state_wkv, state_shift, page_table, c_prompt, c_sample, bias_table, w_ada, b_ada, g_pre_mix, g_post_mix, g_pre_ffn, g_post_ffn, w_in, mu_shift, w0, w_lora_w, a0, w_lora_a, w_lora_g, k_k, k_a, r_k, lnx_g, lnx_b, lam_q1, lam_k1, lam_q2, lam_k2, subln_g, w_out, w_ffn_up, w_ffn_down):
    bsz, seq, d = x_prompt.shape
    dbs, tdec, _ = x_sample.shape
    n_p, n_s = bsz * seq, dbs * tdec

    w_in_b = jnp.concatenate([w_in[0][:, 3 * ATT_WIDTH:], jnp.zeros((d, P_PAD - RWKV_PROJ), F32),
                              w_in[0][:, :3 * ATT_WIDTH]], axis=1).astype(BF16)
    w_out_b = w_out[0].astype(BF16)
    w_up_b = w_ffn_up[0].astype(BF16)
    w_down_b = w_ffn_down[0].astype(BF16)
    wwa = jnp.zeros((128, 2 * RWKV_WIDTH), F32)
    wwa = wwa.at[:64, :RWKV_WIDTH].set(w_lora_w[0]).at[64:, RWKV_WIDTH:].set(w_lora_a[0]).astype(BF16)
    wg = jnp.pad(w_lora_g[0], ((0, P_PAD - GATE_IN - w_lora_g.shape[1]), (0, 0))).astype(BF16)
    head_id = jnp.arange(RWKV_WIDTH) // RWKV_HEAD
    ones_bd = (head_id[:, None] == head_id[None, :]).astype(BF16)
    prep_consts = (_pad_cols(mu_shift, P_PAD), w0, a0, k_k, k_a, r_k.reshape(1, RWKV_WIDTH), wwa, wg, ones_bd)
    lamv = jnp.concatenate([lam_q1, lam_k1, lam_q2, lam_k2], axis=0)

    n_c = bsz + dbs
    c_all = jnp.pad(jnp.concatenate([c_prompt, c_sample], axis=0), ((0, (-n_c) % 8), (0, 0)))
    mod = _ada(c_all, w_ada[0], b_ada)
    mod_p = mod[:bsz].reshape(bsz, 6, 1, d)
    mod_s = jnp.repeat(mod[bsz:n_c].reshape(dbs, 6, d), tdec, axis=0).transpose(1, 0, 2).reshape(1, 6, n_s, d)

    xp = x_prompt.reshape(n_p, d)
    xs = x_sample.reshape(n_s, d)
    tm_p = min(1024, seq)
    proj_p, qkvb_p = _inproj(xp, mod_p, g_pre_mix, w_in_b, tm_p, seq // tm_p)
    proj_s, qkvb_s = _inproj(xs, mod_s, g_pre_mix, w_in_b, n_s, 1)
    k_col, v_col = P_PAD + ATT_WIDTH, P_PAD + 2 * ATT_WIDTH
    k_p, v_p = proj_p[:, k_col:v_col], proj_p[:, v_col:]
    k_s, v_s, p_s = proj_s[:, k_col:v_col], proj_s[:, v_col:], proj_s[:, :P_PAD]

    att_p = _prompt_attn(qkvb_p.reshape(bsz, seq, 3 * ATT_WIDTH), bias_table, lamv, subln_g)
    q_s = (proj_s[:, P_PAD:P_PAD + ATT_WIDTH] * (ATT_SCALE * LOG2E)).astype(BF16)
    q4 = q_s.reshape(dbs, tdec, ATT_HEADS, HEAD_V).transpose(0, 2, 1, 3)
    first = jnp.arange(HEAD_V) < HEAD_QK
    zq = jnp.zeros((dbs, ATT_HEADS, 8 - tdec, HEAD_V), BF16)
    q_rows = jnp.concatenate([jnp.where(first, q4, 0), zq, jnp.where(first, 0, q4), zq], axis=2)
    n_pool = cache_k.shape[1]
    att_s = _sample_attn(q_rows.reshape(dbs, ATT_HEADS * QROWS, HEAD_V),
                         k_s.reshape(dbs, tdec, ATT_WIDTH), v_s.reshape(dbs, tdec, ATT_WIDTH),
                         cache_k[0].reshape(n_pool, PAGE_ROWS, HEAD_V), cache_v[0].reshape(n_pool, PAGE_ROWS, HEAD_V),
                         page_table, bias_table, lamv, subln_g)

    pre_p = _rwkv_prep(proj_p, None, prep_consts, bsz, min(256, seq))
    p_s3 = p_s.reshape(dbs, tdec, P_PAD)
    shift_s = jnp.concatenate([_pad_cols(state_shift[0], P_PAD)[:, None, :], p_s3[:, :-1]], axis=1)
    pre_s = _rwkv_prep(p_s, shift_s.reshape(n_s, P_PAD), prep_consts, dbs, n_s)
    r_p, lw_p, kk_p, vv_p, a_p, b_p, g_p, bonus_p = pre_p
    r_s, lw_s, kk_s, vv_s, a_s, b_s, g_s, bonus_s = pre_s
    seq3 = lambda t: t.reshape(bsz, seq, RWKV_WIDTH)
    y_p, sbd_p = _wkv(seq3(r_p), seq3(lw_p), seq3(kk_p), seq3(vv_p), seq3(a_p), seq3(b_p),
                      jnp.zeros((bsz, RWKV_PAIRS, 128, 128), F32))
    dec3 = lambda t: jnp.pad(t.reshape(dbs, tdec, RWKV_WIDTH), ((0, 0), (0, (-tdec) % 8), (0, 0)))
    y_s, sbd_s = _wkv(dec3(r_s), dec3(lw_s), dec3(kk_s), dec3(vv_s), dec3(a_s), dec3(b_s),
                      _state_to_blockdiag(state_wkv[0]))
    y_s = y_s[:, :tdec].reshape(n_s, RWKV_WIDTH)

    tm_o = min(256, seq)
    x1_p = _outproj(att_p.reshape(n_p, ATT_WIDTH), y_p.reshape(n_p, RWKV_WIDTH), g_p, bonus_p, lnx_g, lnx_b,
                    ones_bd, w_out_b, xp, mod_p, g_post_mix, tm_o, seq // tm_o)
    x1_s = _outproj(att_s.reshape(n_s, ATT_WIDTH), y_s, g_s, bonus_s, lnx_g, lnx_b,
                    ones_bd, w_out_b, xs, mod_s, g_post_mix, n_s, 1)
    tm_f = min(512, seq)
    out_p = _ffn(x1_p, mod_p, g_pre_ffn, g_post_ffn, w_up_b, w_down_b, tm_f, seq // tm_f)
    out_s = _ffn(x1_s, mod_s, g_pre_ffn, g_post_ffn, w_up_b, w_down_b, n_s, 1)

    return (out_p.reshape(bsz, seq, d),
            out_s.reshape(dbs, tdec, d),
            k_p.reshape(1, bsz, seq, ATT_HEADS, HEAD_V),
            v_p.reshape(1, bsz, seq, ATT_HEADS, HEAD_V),
            _blockdiag_to_state(sbd_p)[None],
            proj_p.reshape(bsz, seq, PROJ_W)[None, :, -1, :RWKV_PROJ],
            k_s.reshape(1, dbs, tdec, ATT_HEADS, HEAD_V),
            v_s.reshape(1, dbs, tdec, ATT_HEADS, HEAD_V),
            _blockdiag_to_state(sbd_s)[None],
            p_s3[None, :, -1, :RWKV_PROJ])
```

```python
import functools
import math

import numpy as np
import jax
import jax.numpy as jnp
from jax import lax
from jax.experimental import pallas as pl
from jax.experimental.pallas import tpu as pltpu

F32 = jnp.float32
BF16 = jnp.bfloat16

D_MODEL = 2048
ATT_WIDTH = 1024
RWKV_WIDTH = 1024
ATT_HEADS = 8
HEAD_V = 128
HEAD_QK = 64
RWKV_HEAD = 64
RWKV_PAIRS = RWKV_WIDTH // 128
RWKV_PROJ = 3360
P_PAD = 3584
LORA_IN = 3072
GATE_IN = 3200
FFN_DIM = 8192
N_BUCKETS = 32
MAX_DISTANCE = 128
PAGE = 128
ATT_SCALE = HEAD_QK ** -0.5
RMS_EPS = 1e-6
SUBLN_EPS = 1e-5
LNX_EPS = 64e-5
NEG_INF = -1e30
LOG2E = math.log2(math.e)
LAMBDA_INIT = 0.8 - 0.6 * math.exp(-0.3 * 0)
CHUNK = 64
VMEM_LIMIT = 56 * 1024 * 1024


def _params(*sem):
    return pltpu.CompilerParams(dimension_semantics=sem, vmem_limit_bytes=VMEM_LIMIT)


def _dot(a, b):
    return jnp.dot(a, b, preferred_element_type=F32)


def _dot_nt(a, b):
    return lax.dot_general(a, b, (((1,), (1,)), ((), ())), preferred_element_type=F32)


def _dot_tn(a, b):
    return lax.dot_general(a, b, (((0,), (0,)), ((), ())), preferred_element_type=F32)


def _t5_bucket_np(dist):
    max_exact = N_BUCKETS // 2
    d = np.maximum(dist, 0)
    ratio = np.log(np.maximum(d, 1).astype(np.float32) / max_exact) / math.log(MAX_DISTANCE / max_exact)
    large = np.minimum(max_exact + (ratio * (N_BUCKETS - max_exact)).astype(np.int32), N_BUCKETS - 1)
    return np.where(d < max_exact, d, large).astype(np.int32)


def _table_lookup(bucket, tbl_ref, h):
    out = jnp.zeros(bucket.shape, F32)
    for b in range(N_BUCKETS):
        out = jnp.where(bucket == b, tbl_ref[b, h], out)
    return out


def _lam(lamv_ref):
    v = lamv_ref[...]
    s1 = jnp.sum(v[0:1] * v[1:2], axis=-1, keepdims=True)
    s2 = jnp.sum(v[2:3] * v[3:4], axis=-1, keepdims=True)
    return jnp.exp(s1) - jnp.exp(s2) + LAMBDA_INIT


def _rms(x, g, eps):
    return x * lax.rsqrt(jnp.mean(x * x, axis=-1, keepdims=True) + eps) * g


def _lanes(x, reps):
    return jnp.concatenate([x] * reps, axis=1)


def _ada_kernel(c_ref, w_ref, b_ref, o_ref):
    c = c_ref[...]
    x = (c * jax.nn.sigmoid(c)).astype(BF16)
    o_ref[...] = _dot(x, w_ref[...].astype(BF16)) + b_ref[...]


def _ada(c_all, w_ada, b_ada):
    rows, n = c_all.shape[0], w_ada.shape[1]
    tn = 1536
    return pl.pallas_call(
        _ada_kernel,
        grid=(n // tn,),
        in_specs=[pl.BlockSpec((rows, D_MODEL), lambda j: (0, 0)),
                  pl.BlockSpec((D_MODEL, tn), lambda j: (0, j)),
                  pl.BlockSpec((1, tn), lambda j: (0, j))],
        out_specs=pl.BlockSpec((rows, tn), lambda j: (0, j)),
        out_shape=jax.ShapeDtypeStruct((rows, n), F32),
        compiler_params=_params("arbitrary"),
        name="ada_mod",
    )(c_all, w_ada, b_ada)


IN_TN = 512


PROJ_W = 3 * ATT_WIDTH + P_PAD
NQKV = 3 * ATT_WIDTH // IN_TN
NPJ = P_PAD // IN_TN


def _inproj_kernel(x_ref, mod_ref, g_ref, w_ref, proj_ref, qkvb_ref, h_scr):
    j = pl.program_id(1)

    @pl.when(j == 0)
    def _():
        h = _rms(x_ref[...], g_ref[...], RMS_EPS) * (1.0 + mod_ref[0, 1]) + mod_ref[0, 0]
        h_scr[...] = h.astype(BF16)

    proj_ref[...] = _dot(h_scr[...], w_ref[...])

    @pl.when(j < NQKV)
    def _():
        scale = jnp.where(j < ATT_WIDTH // IN_TN, ATT_SCALE * LOG2E, 1.0)
        qkvb_ref[...] = (proj_ref[...] * scale).astype(BF16)


def _inproj(x, mod4, g_pre, w_in_b, tm, tiles_per_mod):
    m = x.shape[0]
    r = mod4.shape[2]
    nj = PROJ_W // IN_TN
    return pl.pallas_call(
        _inproj_kernel,
        grid=(m // tm, nj),
        in_specs=[pl.BlockSpec((tm, D_MODEL), lambda i, j: (i, 0)),
                  pl.BlockSpec((1, 6, r, D_MODEL), lambda i, j: (i // tiles_per_mod, 0, 0, 0)),
                  pl.BlockSpec((1, D_MODEL), lambda i, j: (0, 0)),
                  pl.BlockSpec((D_MODEL, IN_TN), lambda i, j: (0, j))],
        out_specs=[pl.BlockSpec((tm, IN_TN), lambda i, j: (i, j)),
                   pl.BlockSpec((tm, IN_TN), lambda i, j: (i, jnp.minimum(j, NQKV - 1)))],
        out_shape=[jax.ShapeDtypeStruct((m, PROJ_W), F32),
                   jax.ShapeDtypeStruct((m, 3 * ATT_WIDTH), BF16)],
        scratch_shapes=[pltpu.VMEM((tm, D_MODEL), BF16)],
        compiler_params=_params("arbitrary", "arbitrary"),
        name="in_proj",
    )(x, mod4, g_pre, w_in_b)


ATT_T = 512
ATT_SUB = 128


ATT_HPS = 4


def _pattn_kernel(qi_ref, ki_ref, tbl_ref, lamv_ref, bkt_ref, subg_ref, q_ref, k_ref, v_ref, o_ref,
                  bias_scr, q2_scr, m_scr, l_scr, acc_scr):
    hg = pl.program_id(1)
    step = pl.program_id(2)
    qi = qi_ref[step]
    ki = ki_ref[step]
    nsub = ATT_T // ATT_SUB
    heads = range(ATT_HPS)
    lanes = [slice(u * HEAD_V, (u + 1) * HEAD_V) for u in heads]

    @pl.when(step == 0)
    def _():
        row = lax.broadcasted_iota(jnp.int32, (ATT_SUB, ATT_SUB), 0)
        col = lax.broadcasted_iota(jnp.int32, (ATT_SUB, ATT_SUB), 1)
        for u in heads:
            h = hg * ATT_HPS + u
            t0 = jnp.where(col > row, NEG_INF, _table_lookup(bkt_ref[0], tbl_ref, h) * LOG2E)
            t1 = _table_lookup(bkt_ref[1], tbl_ref, h) * LOG2E
            far = jnp.full((ATT_SUB, ATT_SUB), tbl_ref[N_BUCKETS - 1, h] * LOG2E, F32)
            masked = jnp.full((ATT_SUB, ATT_SUB), NEG_INF, F32)
            for rb in range(nsub):
                for cb in range(nsub):
                    d = rb - cb
                    diag = t0 if d == 0 else t1 if d == 1 else far if d >= 2 else masked
                    off = t1 if (rb == 0 and cb == nsub - 1) else far
                    rs, cs = slice(rb * ATT_SUB, (rb + 1) * ATT_SUB), slice(cb * ATT_SUB, (cb + 1) * ATT_SUB)
                    bias_scr[u, 0, rs, cs] = diag
                    bias_scr[u, 1, rs, cs] = off
                    bias_scr[u, 2, rs, cs] = far

    @pl.when(ki == 0)
    def _():
        lane = lax.broadcasted_iota(jnp.int32, (ATT_T, HEAD_V), 1)
        for u in heads:
            q = q_ref[0, :, lanes[u]]
            zero = jnp.zeros_like(q)
            q2_scr[u, 0:ATT_T, :] = jnp.where(lane < HEAD_QK, q, zero)
            q2_scr[u, ATT_T:, :] = jnp.where(lane >= HEAD_QK, q, zero)
        m_scr[...] = jnp.full(m_scr.shape, NEG_INF, F32)
        l_scr[...] = jnp.zeros(l_scr.shape, F32)
        acc_scr[...] = jnp.zeros(acc_scr.shape, F32)

    which = jnp.minimum(qi - ki, 2)
    s = []
    for u in heads:
        bias = bias_scr[u, which]
        s.append(_dot_nt(q2_scr[u], k_ref[0, :, lanes[u]]) + jnp.concatenate([bias, bias], axis=0))
    m_prev = [m_scr[u] for u in heads]
    m_new = [jnp.maximum(m_prev[u], jnp.max(s[u], axis=-1, keepdims=True)) for u in heads]
    alpha = [jnp.exp2(m_prev[u] - m_new[u]) for u in heads]
    p = [jnp.exp2(s[u] - _lanes(m_new[u], ATT_T // 128)) for u in heads]
    for u in heads:
        l_scr[u] = alpha[u] * l_scr[u] + jnp.sum(p[u], axis=-1, keepdims=True)
        acc_scr[u] = alpha[u] * acc_scr[u] + _dot(p[u].astype(BF16), v_ref[0, :, lanes[u]])
        m_scr[u] = m_new[u]

    @pl.when(ki == qi)
    def _():
        lam = _lam(lamv_ref)
        for u in heads:
            on = acc_scr[u] / l_scr[u]
            o = on[0:ATT_T] - lam * on[ATT_T:]
            o = _rms(o, subg_ref[...], SUBLN_EPS) * (1.0 - LAMBDA_INIT)
            o_ref[0, :, lanes[u]] = o.astype(BF16)


def _prompt_attn(qkv, tbl, lamv, subg):
    b, s, _ = qkv.shape
    nq = s // ATT_T
    r = np.arange(ATT_SUB)
    dist = r[:, None] - r[None, :]
    bkt = jnp.asarray(np.stack([_t5_bucket_np(dist), _t5_bucket_np(dist + ATT_SUB)]))
    pairs = [(qi, ki) for qi in range(nq) for ki in range(qi + 1)]
    qi_of = jnp.asarray(np.array([p[0] for p in pairs], np.int32))
    ki_of = jnp.asarray(np.array([p[1] for p in pairs], np.int32))
    groups = ATT_HEADS // ATT_HPS
    width = ATT_HPS * HEAD_V
    q_spec = pl.BlockSpec((1, ATT_T, width), lambda bi, h, st, qi, ki: (bi, qi[st], h))
    k_spec = pl.BlockSpec((1, ATT_T, width), lambda bi, h, st, qi, ki: (bi, ki[st], groups + h))
    v_spec = pl.BlockSpec((1, ATT_T, width), lambda bi, h, st, qi, ki: (bi, ki[st], 2 * groups + h))
    const = lambda shape: pl.BlockSpec(shape, lambda bi, h, st, qi, ki: (0,) * len(shape))
    grid_spec = pltpu.PrefetchScalarGridSpec(
        num_scalar_prefetch=2,
        grid=(b, groups, len(pairs)),
        in_specs=[pl.BlockSpec(memory_space=pltpu.SMEM), const((4, HEAD_QK)), const((2, ATT_SUB, ATT_SUB)),
                  const((1, HEAD_V)), q_spec, k_spec, v_spec],
        out_specs=q_spec,
        scratch_shapes=[pltpu.VMEM((ATT_HPS, 3, ATT_T, ATT_T), F32),
                        pltpu.VMEM((ATT_HPS, 2 * ATT_T, HEAD_V), BF16),
                        pltpu.VMEM((ATT_HPS, 2 * ATT_T, 128), F32),
                        pltpu.VMEM((ATT_HPS, 2 * ATT_T, 128), F32),
                        pltpu.VMEM((ATT_HPS, 2 * ATT_T, HEAD_V), F32)],
    )
    return pl.pallas_call(
        _pattn_kernel,
        grid_spec=grid_spec,
        out_shape=jax.ShapeDtypeStruct((b, s, ATT_WIDTH), BF16),
        compiler_params=_params("arbitrary", "arbitrary", "arbitrary"),
        name="prompt_attn",
    )(qi_of, ki_of, tbl, lamv, bkt, subg, qkv, qkv, qkv)


PAGES_PER_STEP = 16
QROWS = 16
PAGE_ROWS = PAGE * ATT_HEADS


def _sattn_kernel(pt_ref, tbl_ref, lamv_ref, bkt_ref, subg_ref, q_ref, kn_ref, vn_ref, *rest):
    g_pages = PAGES_PER_STEP
    k_refs = rest[:g_pages]
    v_refs = rest[g_pages:2 * g_pages]
    o_ref, bias_scr, m_scr, l_scr, acc_scr = rest[2 * g_pages:]
    del pt_ref
    b = pl.program_id(0)
    j = pl.program_id(1)
    nj = pl.num_programs(1)
    n_new = kn_ref.shape[1]
    nrow = ATT_HEADS * QROWS

    @pl.when((b == 0) & (j == 0))
    def _():
        key_head = lax.broadcasted_iota(jnp.int32, (QROWS, PAGE_ROWS), 1) % ATT_HEADS
        for h in range(ATT_HEADS):
            rows = slice(h * QROWS, (h + 1) * QROWS)
            bias_scr[0, rows, :] = jnp.where(key_head == h, tbl_ref[N_BUCKETS - 1, h] * LOG2E, NEG_INF)
            bias_scr[1, rows, :] = jnp.where(key_head == h, _table_lookup(bkt_ref[...], tbl_ref, h) * LOG2E, NEG_INF)

    @pl.when(j == 0)
    def _():
        m_scr[...] = jnp.full(m_scr.shape, NEG_INF, F32)
        l_scr[...] = jnp.zeros(l_scr.shape, F32)
        acc_scr[...] = jnp.zeros(acc_scr.shape, F32)

    q = q_ref[0]
    s_pages = []
    for g in range(g_pages):
        bias = bias_scr[(j == nj - 1).astype(jnp.int32)] if g == g_pages - 1 else bias_scr[0]
        s_pages.append(_dot_nt(q, k_refs[g][0].astype(BF16)) + bias)
    m_run = m_scr[...]
    m_new = m_run
    for s in s_pages:
        m_new = jnp.maximum(m_new, jnp.max(s, axis=-1, keepdims=True))
    alpha = jnp.exp2(m_run - m_new)
    m_lanes = _lanes(m_new, PAGE_ROWS // 128)
    l_run = alpha * l_scr[...]
    acc = alpha * acc_scr[...]
    for g in range(g_pages):
        p = jnp.exp2(s_pages[g] - m_lanes)
        l_run = l_run + jnp.sum(p, axis=-1, keepdims=True)
        acc = acc + _dot(p.astype(BF16), v_refs[g][0].astype(BF16))
    m_run = m_new
    m_scr[...] = m_run
    l_scr[...] = l_run
    acc_scr[...] = acc

    @pl.when(j == nj - 1)
    def _():
        lam = _lam(lamv_ref)
        qf = q.astype(F32)
        t_row = lax.broadcasted_iota(jnp.int32, (nrow, 1), 0) % 8
        head_of_row = lax.broadcasted_iota(jnp.int32, (nrow, 1), 0) // QROWS
        m_fin, l_fin, acc_fin = m_run, l_run, acc
        tbl_rows = []
        for dd in range(n_new):
            tbl_row = jnp.zeros((nrow, 1), F32)
            for h in range(ATT_HEADS):
                tbl_row = jnp.where(head_of_row == h, tbl_ref[dd, h] * LOG2E, tbl_row)
            tbl_rows.append(tbl_row)
        for tn in range(n_new):
            d = t_row - tn
            bias = jnp.full((nrow, 1), NEG_INF, F32)
            for dd in range(n_new):
                bias = jnp.where(d == dd, tbl_rows[dd], bias)
            k_rows = jnp.concatenate(
                [jnp.broadcast_to(kn_ref[0, tn:tn + 1, h * HEAD_V:(h + 1) * HEAD_V], (QROWS, HEAD_V))
                 for h in range(ATT_HEADS)], axis=0)
            v_rows = jnp.concatenate(
                [jnp.broadcast_to(vn_ref[0, tn:tn + 1, h * HEAD_V:(h + 1) * HEAD_V], (QROWS, HEAD_V))
                 for h in range(ATT_HEADS)], axis=0)
            s = jnp.sum(qf * k_rows, axis=-1, keepdims=True) + bias
            m_new = jnp.maximum(m_fin, s)
            alpha = jnp.exp2(m_fin - m_new)
            p = jnp.exp2(s - m_new)
            l_fin = alpha * l_fin + p
            acc_fin = alpha * acc_fin + p * v_rows
            m_fin = m_new
        on = acc_fin / l_fin
        for h in range(ATT_HEADS):
            o = on[h * QROWS:h * QROWS + 8] - lam * on[h * QROWS + 8:(h + 1) * QROWS]
            o = _rms(o, subg_ref[...], SUBLN_EPS) * (1.0 - LAMBDA_INIT)
            o_ref[0, :, h * HEAD_V:(h + 1) * HEAD_V] = o[0:n_new]


def _sample_attn(q_rows, k_new, v_new, cache_k, cache_v, page_table, tbl, lamv, subg):
    db, t_new = k_new.shape[0], k_new.shape[1]
    n_pages = page_table.shape[1]
    g_pages = PAGES_PER_STEP
    nrow = ATT_HEADS * QROWS
    t_row = (np.arange(QROWS) % 8)[:, None]
    tok = (np.arange(PAGE_ROWS) // ATT_HEADS)[None, :]
    bkt = jnp.asarray(_t5_bucket_np(PAGE + t_row - tok))

    def page_spec(g):
        return pl.BlockSpec((1, PAGE_ROWS, HEAD_V), lambda b, j, pt: (pt[b, j * g_pages + g], 0, 0))

    new_spec = pl.BlockSpec((1, t_new, ATT_WIDTH), lambda b, j, pt: (b, 0, 0))
    grid_spec = pltpu.PrefetchScalarGridSpec(
        num_scalar_prefetch=1,
        grid=(db, n_pages // g_pages),
        in_specs=[pl.BlockSpec(memory_space=pltpu.SMEM),
                  pl.BlockSpec((4, HEAD_QK), lambda b, j, pt: (0, 0)),
                  pl.BlockSpec((QROWS, PAGE_ROWS), lambda b, j, pt: (0, 0)),
                  pl.BlockSpec((1, HEAD_V), lambda b, j, pt: (0, 0)),
                  pl.BlockSpec((1, nrow, HEAD_V), lambda b, j, pt: (b, 0, 0)),
                  new_spec, new_spec]
                 + [page_spec(g) for g in range(g_pages)]
                 + [page_spec(g) for g in range(g_pages)],
        out_specs=new_spec,
        scratch_shapes=[pltpu.VMEM((2, nrow, PAGE_ROWS), F32),
                        pltpu.VMEM((nrow, 128), F32),
                        pltpu.VMEM((nrow, 128), F32),
                        pltpu.VMEM((nrow, HEAD_V), F32)],
    )
    return pl.pallas_call(
        _sattn_kernel,
        grid_spec=grid_spec,
        out_shape=jax.ShapeDtypeStruct((db, t_new, ATT_WIDTH), F32),
        compiler_params=_params("arbitrary", "arbitrary"),
        name="sample_attn",
    )(page_table, tbl, lamv, bkt, subg, q_rows, k_new, v_new,
      *([cache_k] * g_pages), *([cache_v] * g_pages))


def _prep_math(p, ps, mu, w0, a0, kkw, kaw, rkw, wwa, wg, ones_bd, out_refs):
    pm = p + mu * (ps - p)
    r = pm[:, 0:RWKV_WIDTH]
    kr = pm[:, RWKV_WIDTH:2 * RWKV_WIDTH]
    v = pm[:, 2 * RWKV_WIDTH:3 * RWKV_WIDTH]
    wa = pm[:, LORA_IN:LORA_IN + 128]
    lane = lax.broadcasted_iota(jnp.int32, wa.shape, 1)
    la = _dot(jnp.where(lane < 64, jnp.tanh(wa), wa).astype(BF16), wwa)
    z = -(w0 + la[:, :RWKV_WIDTH])
    softplus = jnp.maximum(z, 0.0) + jnp.log(1.0 + jnp.exp(-jnp.abs(z)))
    log_decay = -jnp.exp(-softplus - 0.5)
    asig = jax.nn.sigmoid(a0 + la[:, RWKV_WIDTH:])
    g = _dot(jax.nn.sigmoid(pm[:, GATE_IN:P_PAD]).astype(BF16), wg)
    kk = kr * kkw
    norm = jnp.sqrt(_dot((kk * kk).astype(BF16), ones_bd))
    kk = kk / jnp.maximum(norm, 1e-12)
    k2 = kr * (1.0 + (asig - 1.0) * kaw)
    bonus = _dot((r * k2 * rkw).astype(BF16), ones_bd) * v
    r_ref, lw_ref, k_ref, v_ref, a_ref, b_ref, g_ref, bonus_ref = out_refs
    r_ref[...] = r
    lw_ref[...] = log_decay
    k_ref[...] = k2
    v_ref[...] = v
    a_ref[...] = -kk
    b_ref[...] = kk * asig
    g_ref[...] = g
    bonus_ref[...] = bonus


def _prep_carry_kernel(*refs):
    p_refs, (mu, w0, a0, kkw, kaw, rkw, wwa, wg, ones_bd) = refs[:NPJ], refs[NPJ:NPJ + 9]
    out_refs, carry = refs[NPJ + 9:NPJ + 17], refs[NPJ + 17]
    i = pl.program_id(1)

    @pl.when(i == 0)
    def _():
        carry[...] = jnp.zeros(carry.shape, F32)

    p = jnp.concatenate([r[...] for r in p_refs], axis=1)
    row = lax.broadcasted_iota(jnp.int32, p.shape, 0)
    ps = jnp.where(row == 0, carry[...], pltpu.roll(p, 1, 0))
    carry[...] = p[p.shape[0] - 1:, :]
    _prep_math(p, ps, mu[...], w0[...], a0[...], kkw[...], kaw[...], rkw[...], wwa[...], wg[...],
               ones_bd[...], out_refs)


def _prep_shift_kernel(p_ref, ps_ref, mu, w0, a0, kkw, kaw, rkw, wwa, wg, ones_bd, *out_refs):
    _prep_math(p_ref[...], ps_ref[...], mu[...], w0[...], a0[...], kkw[...], kaw[...], rkw[...], wwa[...],
               wg[...], ones_bd[...], out_refs)


def _rwkv_prep(p, p_shift, consts, n_seq, tm):
    m = p.shape[0]
    per_seq = m // n_seq // tm if p_shift is None else 0
    if p_shift is None:
        grid = (n_seq, per_seq)
        tile = lambda w: pl.BlockSpec((tm, w), lambda s, i: (s * per_seq + i, 0))
        const = lambda a: pl.BlockSpec(a.shape, lambda s, i: (0,) * a.ndim)
        kern, args, sem = _prep_carry_kernel, (p,) * NPJ, ("arbitrary", "arbitrary")
        in_tiles = [pl.BlockSpec((tm, IN_TN), functools.partial(lambda s, i, c: (s * per_seq + i, NQKV + c), c=c))
                    for c in range(NPJ)]
        scratch = [pltpu.VMEM((1, P_PAD), F32)]
    else:
        grid = (m // tm,)
        tile = lambda w: pl.BlockSpec((tm, w), lambda i: (i, 0))
        const = lambda a: pl.BlockSpec(a.shape, lambda i: (0,) * a.ndim)
        kern, args, sem = _prep_shift_kernel, (p, p_shift), ("arbitrary",)
        in_tiles = [tile(P_PAD)] * 2
        scratch = []
    return pl.pallas_call(
        kern,
        grid=grid,
        in_specs=in_tiles + [const(a) for a in consts],
        out_specs=[tile(RWKV_WIDTH)] * 8,
        out_shape=[jax.ShapeDtypeStruct((m, RWKV_WIDTH), F32)] * 8,
        scratch_shapes=scratch,
        compiler_params=_params(*sem),
        name="rwkv_prep",
    )(*args, *consts)


def _wkv_kernel(r_ref, lw_ref, k_ref, v_ref, a_ref, b_ref, s0_ref, y_ref, sout_ref, s_scr):
    c = pl.program_id(1)
    nc = pl.num_programs(1)
    C = CHUNK
    pairs = range(RWKV_PAIRS)

    @pl.when(c == 0)
    def _():
        s_scr[...] = s0_ref[0]

    t_in = lw_ref.shape[1]

    def load(ref):
        x = ref[0]
        return x if t_in == C else jnp.concatenate([x, jnp.zeros((C - t_in, RWKV_WIDTH), F32)], axis=0)

    r_in, lw, k_in, v_all, a_in, b_in = (load(ref) for ref in (r_ref, lw_ref, k_ref, v_ref, a_ref, b_ref))
    trow = lax.broadcasted_iota(jnp.int32, (C, C), 0)
    tcol = lax.broadcasted_iota(jnp.int32, (C, C), 1)
    tri = jnp.where(tcol <= trow, 1.0, 0.0).astype(BF16)
    h1 = lw.astype(BF16)
    r1 = lw - h1.astype(F32)
    h2 = r1.astype(BF16)
    h3 = (r1 - h2.astype(F32)).astype(BF16)
    cs = _dot(tri, jnp.concatenate([h1, h2, h3], axis=1))
    cum = cs[:, :RWKV_WIDTH] + cs[:, RWKV_WIDTH:2 * RWKV_WIDTH] + cs[:, 2 * RWKV_WIDTH:]
    tot = cum[C - 1:C, :]
    e_inv = jnp.exp(-cum)
    e_tail = jnp.exp(tot - cum)
    at_all = a_in * jnp.exp(cum - lw)
    rt_all = r_in * jnp.exp(cum)
    bt_all = b_in * e_inv
    kt_all = k_in * e_inv
    bh_all = b_in * e_tail
    kh_all = k_in * e_tail
    w_tot = jnp.exp(tot)

    row = lax.broadcasted_iota(jnp.int32, (2 * C, 2 * C), 0)
    col = lax.broadcasted_iota(jnp.int32, (2 * C, 2 * C), 1)
    same = (row // C) == (col // C)
    strict = same & (col < row)
    incl = same & (col <= row)
    eye = jnp.where(row == col, 1.0, 0.0)
    lane = lax.broadcasted_iota(jnp.int32, (C, 128), 1)
    first = lane < RWKV_HEAD
    bf = lambda x: x.astype(BF16)
    ls = [slice(pi * 128, (pi + 1) * 128) for pi in pairs]

    def stack(x):
        return jnp.concatenate([jnp.where(first, x, 0.0), jnp.where(first, 0.0, x)], axis=0)

    def dup(x):
        return jnp.concatenate([x, x], axis=0)

    at_s = [stack(at_all[:, s]) for s in ls]
    rt_s = [stack(rt_all[:, s]) for s in ls]
    v_s = [stack(v_all[:, s]) for s in ls]
    g = [_dot_nt(bf(jnp.concatenate([at_s[i], rt_s[i]], axis=0)),
                 bf(jnp.concatenate([dup(bt_all[:, ls[i]]), dup(kt_all[:, ls[i]])], axis=0))) for i in pairs]
    l_ab = [jnp.where(strict, g[i][0:2 * C, 0:2 * C], 0.0) for i in pairs]
    a_ak = [jnp.where(strict, g[i][0:2 * C, 2 * C:4 * C], 0.0) for i in pairs]
    a_rb = [jnp.where(incl, g[i][2 * C:4 * C, 0:2 * C], 0.0) for i in pairs]
    a_rk = [jnp.where(incl, g[i][2 * C:4 * C, 2 * C:4 * C], 0.0) for i in pairs]
    x = [eye + l for l in l_ab]
    pw = l_ab
    for _ in range(int(math.log2(C)) - 1):
        pw = [_dot(bf(m), bf(m)) for m in pw]
        x = [x[i] + _dot(bf(x[i]), bf(pw[i])) for i in pairs]
    av = [_dot(bf(jnp.concatenate([a_ak[i], a_rk[i]], axis=0)), bf(v_s[i])) for i in pairs]
    tx = [_dot(bf(x[i]), bf(jnp.concatenate([at_s[i], av[i][0:2 * C]], axis=1))) for i in pairs]
    s_old = [s_scr[i] for i in pairs]
    az = [_dot_nt(bf(jnp.concatenate([tx[i][:, 0:128], rt_s[i]], axis=0)), bf(s_old[i])) for i in pairs]
    u = [az[i][0:2 * C] + tx[i][:, 128:256] for i in pairs]
    y = [az[i][2 * C:4 * C] + _dot(bf(a_rb[i]), bf(u[i])) + av[i][2 * C:4 * C] for i in pairs]
    for i in pairs:
        y_ref[0, :, ls[i]] = (y[i][0:C] + y[i][C:2 * C])[0:t_in]
    upd =[_dot_tn(bf(jnp.concatenate([u[i], v_s[i]], axis=0)),
                   bf(jnp.concatenate([stack(bh_all[:, ls[i]]), stack(kh_all[:, ls[i]])], axis=0))) for i in pairs]
    for i in pairs:
        s_scr[i] = s_old[i] * w_tot[:, ls[i]] + upd[i]

    @pl.when(c == nc - 1)
    def _():
        sout_ref[0] = s_scr[...]


def _wkv(r, lw, k, v, a, b, s0_bd):
    n_seq, t, _ = r.shape
    t_blk = min(t, CHUNK)
    seq_spec = pl.BlockSpec((1, t_blk, RWKV_WIDTH), lambda s, c: (s, c, 0))
    st_spec = pl.BlockSpec((1, RWKV_PAIRS, 128, 128), lambda s, c: (s, 0, 0, 0))
    return pl.pallas_call(
        _wkv_kernel,
        grid=(n_seq, t // t_blk),
        in_specs=[seq_spec] * 6 + [st_spec],
        out_specs=[seq_spec, st_spec],
        out_shape=[jax.ShapeDtypeStruct((n_seq, t, RWKV_WIDTH), F32),
                   jax.ShapeDtypeStruct((n_seq, RWKV_PAIRS, 128, 128), F32)],
        scratch_shapes=[pltpu.VMEM((RWKV_PAIRS, 128, 128), F32)],
        compiler_params=_params("arbitrary", "arbitrary"),
        name="wkv_scan",
    )(r, lw, k, v, a, b, s0_bd)


def _outproj_kernel(att_ref, y_ref, g_ref, bonus_ref, lng_ref, lnb_ref, ones_ref, wo_ref, x_ref, mod_ref,
                    gpost_ref, o_ref):
    y = y_ref[...]
    ones_bd = ones_ref[...]
    inv_n = 1.0 / RWKV_HEAD
    yc = y - _dot(y.astype(BF16), ones_bd) * inv_n
    var = _dot((yc * yc).astype(BF16), ones_bd) * inv_n
    yn = yc * lax.rsqrt(var + LNX_EPS)
    rw = (yn * lng_ref[...] + lnb_ref[...] + bonus_ref[...]) * g_ref[...]
    mix = (_dot(att_ref[...].astype(BF16), wo_ref[0:ATT_WIDTH, :])
           + _dot(rw.astype(BF16), wo_ref[ATT_WIDTH:, :]))
    o_ref[...] = x_ref[...] + mod_ref[0, 2] * _rms(mix, gpost_ref[...], RMS_EPS)


def _outproj(att, y, g, bonus, lng, lnb, ones_bd, w_out_b, x, mod4, g_post, tm, tiles_per_mod):
    m = x.shape[0]
    r = mod4.shape[2]
    tile = lambda w: pl.BlockSpec((tm, w), lambda i: (i, 0))
    const = lambda a: pl.BlockSpec(a.shape, lambda i: (0,) * a.ndim)
    return pl.pallas_call(
        _outproj_kernel,
        grid=(m // tm,),
        in_specs=[tile(ATT_WIDTH)] * 4 + [const(lng), const(lnb), const(ones_bd), const(w_out_b),
                                          tile(D_MODEL),
                                          pl.BlockSpec((1, 6, r, D_MODEL), lambda i: (i // tiles_per_mod, 0, 0, 0)),
                                          const(g_post)],
        out_specs=tile(D_MODEL),
        out_shape=jax.ShapeDtypeStruct((m, D_MODEL), F32),
        compiler_params=_params("arbitrary"),
        name="out_proj",
    )(att, y, g, bonus, lng, lnb, ones_bd, w_out_b, x, mod4, g_post)


FFN_TF = 1024


def _ffn_kernel(x_ref, mod_ref, gpre_ref, gpost_ref, wu_ref, wd_ref, o_ref, h_scr, acc_scr):
    f = pl.program_id(1)

    @pl.when(f == 0)
    def _():
        h = _rms(x_ref[...], gpre_ref[...], RMS_EPS) * (1.0 + mod_ref[0, 4]) + mod_ref[0, 3]
        h_scr[...] = h.astype(BF16)
        acc_scr[...] = jnp.zeros(acc_scr.shape, F32)

    u = jnp.maximum(_dot(h_scr[...], wu_ref[...]), 0.0)
    acc_scr[...] += _dot((u * u).astype(BF16), wd_ref[...])

    @pl.when(f == pl.num_programs(1) - 1)
    def _():
        o_ref[...] = x_ref[...] + mod_ref[0, 5] * _rms(acc_scr[...], gpost_ref[...], RMS_EPS)


def _ffn(x, mod4, g_pre, g_post, w_up_b, w_down_b, tm, tiles_per_mod):
    m = x.shape[0]
    r = mod4.shape[2]
    return pl.pallas_call(
        _ffn_kernel,
        grid=(m // tm, FFN_DIM // FFN_TF),
        in_specs=[pl.BlockSpec((tm, D_MODEL), lambda i, f: (i, 0)),
                  pl.BlockSpec((1, 6, r, D_MODEL), lambda i, f: (i // tiles_per_mod, 0, 0, 0)),
                  pl.BlockSpec((1, D_MODEL), lambda i, f: (0, 0)),
                  pl.BlockSpec((1, D_MODEL), lambda i, f: (0, 0)),
                  pl.BlockSpec((D_MODEL, FFN_TF), lambda i, f: (0, f)),
                  pl.BlockSpec((FFN_TF, D_MODEL), lambda i, f: (f, 0))],
        out_specs=pl.BlockSpec((tm, D_MODEL), lambda i, f: (i, 0)),
        out_shape=jax.ShapeDtypeStruct((m, D_MODEL), F32),
        scratch_shapes=[pltpu.VMEM((tm, D_MODEL), BF16), pltpu.VMEM((tm, D_MODEL), F32)],
        compiler_params=_params("arbitrary", "arbitrary"),
        name="ffn",
    )(x, mod4, g_pre, g_post, w_up_b, w_down_b)


def _state_to_blockdiag(s):
    n = s.shape[0]
    s = s.reshape(n, RWKV_PAIRS, 2, RWKV_HEAD, RWKV_HEAD)
    z = jnp.zeros((n, RWKV_PAIRS, RWKV_HEAD, RWKV_HEAD), s.dtype)
    top = jnp.concatenate([s[:, :, 0], z], axis=-1)
    bot = jnp.concatenate([z, s[:, :, 1]], axis=-1)
    return jnp.concatenate([top, bot], axis=-2)


def _blockdiag_to_state(sbd):
    n = sbd.shape[0]
    s = jnp.stack([sbd[:, :, :RWKV_HEAD, :RWKV_HEAD], sbd[:, :, RWKV_HEAD:, RWKV_HEAD:]], axis=2)
    return s.reshape(n, 2 * RWKV_PAIRS, RWKV_HEAD, RWKV_HEAD)


def _pad_cols(a, width):
    return jnp.pad(a, ((0, 0), (0, width - a.shape[1])))


def kernel(x_prompt, x_sample, cache_k, cache_v, state_wkv, state_shift, page_table, c_prompt, c_sample, bias_table, w_ada, b_ada, g_pre_mix, g_post_mix, g_pre_ffn, g_post_ffn, w_in, mu_shift, w0, w_lora_w, a0, w_lora_a, w_lora_g, k_k, k_a, r_k, lnx_g, lnx_b, lam_q1, lam_k1, lam_q2, lam_k2, subln_g, w_out, w_ffn_up, w_ffn_down):
    bsz, seq, d = x_prompt.shape
    dbs, tdec, _ = x_sample.shape
    n_p, n_s = bsz * seq, dbs * tdec

    w_in_b = jnp.concatenate([w_in[0], jnp.zeros((d, P_PAD - RWKV_PROJ), F32)], axis=1).astype(BF16)
    w_out_b = w_out[0].astype(BF16)
    w_up_b = w_ffn_up[0].astype(BF16)
    w_down_b = w_ffn_down[0].astype(BF16)
    wwa = jnp.zeros((128, 2 * RWKV_WIDTH), F32)
    wwa = wwa.at[:64, :RWKV_WIDTH].set(w_lora_w[0]).at[64:, RWKV_WIDTH:].set(w_lora_a[0]).astype(BF16)
    wg = jnp.pad(w_lora_g[0], ((0, P_PAD - GATE_IN - w_lora_g.shape[1]), (0, 0))).astype(BF16)
    head_id = jnp.arange(RWKV_WIDTH) // RWKV_HEAD
    ones_bd = (head_id[:, None] == head_id[None, :]).astype(BF16)
    prep_consts = (_pad_cols(mu_shift, P_PAD), w0, a0, k_k, k_a, r_k.reshape(1, RWKV_WIDTH), wwa, wg, ones_bd)
    lamv = jnp.concatenate([lam_q1, lam_k1, lam_q2, lam_k2], axis=0)

    n_c = bsz + dbs
    c_all = jnp.pad(jnp.concatenate([c_prompt, c_sample], axis=0), ((0, (-n_c) % 8), (0, 0)))
    mod = _ada(c_all, w_ada[0], b_ada)
    mod_p = mod[:bsz].reshape(bsz, 6, 1, d)
    mod_s = jnp.repeat(mod[bsz:n_c].reshape(dbs, 6, d), tdec, axis=0).transpose(1, 0, 2).reshape(1, 6, n_s, d)

    xp = x_prompt.reshape(n_p, d)
    xs = x_sample.reshape(n_s, d)
    tm_p = min(1024, seq)
    proj_p, qkvb_p = _inproj(xp, mod_p, g_pre_mix, w_in_b, tm_p, seq // tm_p)
    proj_s, qkvb_s = _inproj(xs, mod_s, g_pre_mix, w_in_b, n_s, 1)
    k_col, v_col, p_col = ATT_WIDTH, 2 * ATT_WIDTH, 3 * ATT_WIDTH
    k_p, v_p = proj_p[:, k_col:v_col], proj_p[:, v_col:p_col]
    k_s, v_s, p_s = proj_s[:, k_col:v_col], proj_s[:, v_col:p_col], proj_s[:, p_col:]

    att_p = _prompt_attn(qkvb_p.reshape(bsz, seq, 3 * ATT_WIDTH), bias_table, lamv, subln_g)
    q_s = (proj_s[:, :ATT_WIDTH] * (ATT_SCALE * LOG2E)).astype(BF16)
    q4 = q_s.reshape(dbs, tdec, ATT_HEADS, HEAD_V).transpose(0, 2, 1, 3)
    first = jnp.arange(HEAD_V) < HEAD_QK
    zq = jnp.zeros((dbs, ATT_HEADS, 8 - tdec, HEAD_V), BF16)
    q_rows = jnp.concatenate([jnp.where(first, q4, 0), zq, jnp.where(first, 0, q4), zq], axis=2)
    n_pool = cache_k.shape[1]
    att_s = _sample_attn(q_rows.reshape(dbs, ATT_HEADS * QROWS, HEAD_V),
                         k_s.reshape(dbs, tdec, ATT_WIDTH), v_s.reshape(dbs, tdec, ATT_WIDTH),
                         cache_k[0].reshape(n_pool, PAGE_ROWS, HEAD_V), cache_v[0].reshape(n_pool, PAGE_ROWS, HEAD_V),
                         page_table, bias_table, lamv, subln_g)

    pre_p = _rwkv_prep(proj_p, None, prep_consts, bsz, min(256, seq))
    p_s3 = p_s.reshape(dbs, tdec, P_PAD)
    shift_s = jnp.concatenate([_pad_cols(state_shift[0], P_PAD)[:, None, :], p_s3[:, :-1]], axis=1)
    pre_s = _rwkv_prep(p_s, shift_s.reshape(n_s, P_PAD), prep_consts, dbs, n_s)
    r_p, lw_p, kk_p, vv_p, a_p, b_p, g_p, bonus_p = pre_p
    r_s, lw_s, kk_s, vv_s, a_s, b_s, g_s, bonus_s = pre_s
    seq3 = lambda t: t.reshape(bsz, seq, RWKV_WIDTH)
    y_p, sbd_p = _wkv(seq3(r_p), seq3(lw_p), seq3(kk_p), seq3(vv_p), seq3(a_p), seq3(b_p),
                      jnp.zeros((bsz, RWKV_PAIRS, 128, 128), F32))
    dec3 = lambda t: jnp.pad(t.reshape(dbs, tdec, RWKV_WIDTH), ((0, 0), (0, (-tdec) % 8), (0, 0)))
    y_s, sbd_s = _wkv(dec3(r_s), dec3(lw_s), dec3(kk_s), dec3(vv_s), dec3(a_s), dec3(b_s),
                      _state_to_blockdiag(state_wkv[0]))
    y_s = y_s[:, :tdec].reshape(n_s, RWKV_WIDTH)

    tm_o = min(256, seq)
    x1_p = _outproj(att_p.reshape(n_p, ATT_WIDTH), y_p.reshape(n_p, RWKV_WIDTH), g_p, bonus_p, lnx_g, lnx_b,
                    ones_bd, w_out_b, xp, mod_p, g_post_mix, tm_o, seq // tm_o)
    x1_s = _outproj(att_s.reshape(n_s, ATT_WIDTH), y_s, g_s, bonus_s, lnx_g, lnx_b,
                    ones_bd, w_out_b, xs, mod_s, g_post_mix, n_s, 1)
    tm_f = min(512, seq)
    out_p = _ffn(x1_p, mod_p, g_pre_ffn, g_post_ffn, w_up_b, w_down_b, tm_f, seq // tm_f)
    out_s = _ffn(x1_s, mod_s, g_pre_ffn, g_post_ffn, w_up_b, w_down_b, n_s, 1)

    return (out_p.reshape(bsz, seq, d),
            out_s.reshape(dbs, tdec, d),
            k_p.reshape(1, bsz, seq, ATT_HEADS, HEAD_V),
            v_p.reshape(1, bsz, seq, ATT_HEADS, HEAD_V),
            _blockdiag_to_state(sbd_p)[None],
            proj_p.reshape(bsz, seq, PROJ_W)[None, :, -1, p_col:p_col + RWKV_PROJ],
            k_s.reshape(1, dbs, tdec, ATT_HEADS, HEAD_V),
            v_s.reshape(1, dbs, tdec, ATT_HEADS, HEAD_V),
            _blockdiag_to_state(sbd_s)[None],
            p_s3[None, :, -1, :RWKV_PROJ])
```
